```python
import jax, jax.numpy as jnp
from jax import lax
import numpy as np

D_MODEL = 1024
BATCH = 16
SEQ = 2048
DEPTH = 2
DEC_BATCH = 16
DEC_SEQ = 4096
PAST_LEN = 128

GRID_W = 64
BLOCK = 128
HEAD_DIM = 64
CONV_DIM = D_MODEL // 2
CONV_WIDTH = 31
WIN_HEADS = (D_MODEL // 2) // HEAD_DIM
WIN_KV_HEADS = 2
WINDOW = 128
ROT_DIM_B = HEAD_DIM // 4
ROPE_THETA_B = 500000.0
AX_HEADS = D_MODEL // HEAD_DIM
AX_KV_HEADS = 4
ROPE_THETA_C = 10000.0
N_EXPERTS = 16
D_EXPERT = D_MODEL
CAPACITY_FACTOR = 2
ALPHA = (2 * DEPTH) ** 0.25
BETA = (8 * DEPTH) ** -0.25
N_EVEN = (DEPTH + 1) // 2
N_ODD = DEPTH // 2
AB_Q = WIN_HEADS * HEAD_DIM
AB_KV = WIN_KV_HEADS * HEAD_DIM
AB_IN = 2 * CONV_DIM + AB_Q + 2 * AB_KV
C_Q = AX_HEADS * HEAD_DIM
C_KV = AX_KV_HEADS * HEAD_DIM
C_IN = C_Q + 2 * C_KV
LN_EPS = 1e-5
RMS_EPS = 1e-6

kernel_name = 'hybrid_conv_window_axial_ec_encoder'


def layer_norm(x, g, b):
    xf = x.astype(jnp.float32)
    mu = jnp.mean(xf, -1, keepdims=True)
    var = jnp.mean(jnp.square(xf - mu), -1, keepdims=True)
    return ((xf - mu) * lax.rsqrt(var + LN_EPS)).astype(x.dtype) * g + b


def rms_norm(x, g):
    xf = x.astype(jnp.float32)
    return (xf * lax.rsqrt(jnp.mean(xf * xf, -1, keepdims=True) + RMS_EPS)).astype(x.dtype) * g


def rope_angles(pos, dim, theta):
    inv = theta ** (-jnp.arange(0, dim, 2, dtype=jnp.float32) / dim)
    return pos[:, None] * inv[None, :]


def apply_rope(x, ang):
    half = x.shape[-1] // 2
    cos = jnp.cos(ang)[None, :, None, :].astype(x.dtype)
    sin = jnp.sin(ang)[None, :, None, :].astype(x.dtype)
    x1, x2 = x[..., :half], x[..., half:]
    return jnp.concatenate([x1 * cos - x2 * sin, x2 * cos + x1 * sin], axis=-1)


def conv_branch(u, gate, w, b, g, beta):
    h = u * jax.nn.sigmoid(gate)
    pad = CONV_WIDTH // 2
    h = lax.conv_general_dilated(h, w[:, None, :].astype(h.dtype), window_strides=(1,),
                                 padding=[(pad, pad)], dimension_numbers=('NWC', 'WIO', 'NWC'),
                                 feature_group_count=CONV_DIM) + b
    return jax.nn.silu(layer_norm(h, g, beta))


def window_attention(q, k, v, sink):
    B, S, H, Dh = q.shape
    KV = k.shape[2]
    G = H // KV
    nb = S // BLOCK
    qb = q.reshape(B, nb, BLOCK, KV, G, Dh)

    def band(t):
        tp = jnp.pad(t, ((0, 0), (BLOCK, BLOCK), (0, 0), (0, 0))).reshape(B, nb + 2, BLOCK, KV, Dh)
        return jnp.concatenate([tp[:, :-2], tp[:, 1:-1], tp[:, 2:]], axis=2)

    kb, vb = band(k), band(v)
    s = jnp.einsum('bnqkgd,bnskd->bnkgqs', qb, kb).astype(jnp.float32) * (Dh ** -0.5)
    blk = jnp.arange(nb)[:, None, None] * BLOCK
    qpos = blk + jnp.arange(BLOCK)[None, :, None]
    kpos = blk - BLOCK + jnp.arange(3 * BLOCK)[None, None, :]
    mask = (jnp.abs(kpos - qpos) <= WINDOW) & (kpos >= 0) & (kpos < S)
    s = jnp.where(mask[None, :, None, None], s, -jnp.inf)
    sk = sink.astype(jnp.float32).reshape(KV, G)[None, None, :, :, None, None]
    m = jnp.maximum(jnp.max(s, -1, keepdims=True), sk)
    p = jnp.exp(s - m)
    p = (p / (jnp.sum(p, -1, keepdims=True) + jnp.exp(sk - m))).astype(v.dtype)
    o = jnp.einsum('bnkgqs,bnskd->bnqkgd', p, vb)
    return o.reshape(B, S, H * Dh)


def blocked_attention(q, k, v):
    B, S, H, Dh = q.shape
    KV = k.shape[2]
    G = H // KV
    nb = S // BLOCK
    qb = jnp.moveaxis(q.reshape(B, nb, BLOCK, KV, G, Dh), 1, 0)
    scale = Dh ** -0.5

    def attend(qi):
        s = jnp.einsum('bqkgd,bskd->bkgqs', qi, k).astype(jnp.float32) * scale
        p = jax.nn.softmax(s, axis=-1).astype(v.dtype)
        return jnp.einsum('bkgqs,bskd->bqkgd', p, v)

    o = lax.map(attend, qb)
    return jnp.moveaxis(o, 0, 1).reshape(B, S, H * Dh)


def mixer_conv_window(x, w_in, conv_w, conv_b, cln_g, cln_b, sink, w_out, ang):
    B, S, _ = x.shape
    h = x @ w_in
    u, gate, q, k, v = jnp.split(h, [CONV_DIM, 2 * CONV_DIM, 2 * CONV_DIM + AB_Q,
                                     2 * CONV_DIM + AB_Q + AB_KV], axis=-1)
    conv_out = conv_branch(u, gate, conv_w, conv_b, cln_g, cln_b)
    q = q.reshape(B, S, WIN_HEADS, HEAD_DIM)
    k = k.reshape(B, S, WIN_KV_HEADS, HEAD_DIM)
    v = v.reshape(B, S, WIN_KV_HEADS, HEAD_DIM)
    q = jnp.concatenate([apply_rope(q[..., :ROT_DIM_B], ang), q[..., ROT_DIM_B:]], axis=-1)
    k = jnp.concatenate([apply_rope(k[..., :ROT_DIM_B], ang), k[..., ROT_DIM_B:]], axis=-1)
    attn_out = window_attention(q, k, v, sink)
    return jnp.concatenate([conv_out, attn_out], axis=-1) @ w_out


def mixer_axial(x, w_in, q_norm, k_norm, w_out, ang_row, ang_col):
    B, S, _ = x.shape
    q, k, v = jnp.split(x @ w_in, [C_Q, C_Q + C_KV], axis=-1)
    q = rms_norm(q.reshape(B, S, AX_HEADS, HEAD_DIM), q_norm)
    k = rms_norm(k.reshape(B, S, AX_KV_HEADS, HEAD_DIM), k_norm)
    v = v.reshape(B, S, AX_KV_HEADS, HEAD_DIM)
    hd = HEAD_DIM // 2
    q = jnp.concatenate([apply_rope(q[..., :hd], ang_row), apply_rope(q[..., hd:], ang_col)], axis=-1)
    k = jnp.concatenate([apply_rope(k[..., :hd], ang_row), apply_rope(k[..., hd:], ang_col)], axis=-1)
    return blocked_attention(q, k, v) @ w_out


def expert_choice_ffn(x, w_r, w_gate, w_up, w_down):
    B, S, D = x.shape
    n = B * S
    cap = CAPACITY_FACTOR * n // N_EXPERTS
    xt = x.reshape(n, D)
    aff = jax.nn.softmax((xt @ w_r).astype(jnp.float32), axis=-1)
    g, idx = lax.top_k(aff.T, cap)
    xe = xt[idx]
    h = jax.nn.silu(jnp.einsum('ecd,edf->ecf', xe, w_gate)) * jnp.einsum('ecd,edf->ecf', xe, w_up)
    ye = jnp.einsum('ecf,efd->ecd', h, w_down) * g[..., None].astype(xe.dtype)
    y = jnp.zeros_like(xt).at[idx.reshape(-1)].add(ye.reshape(-1, D).astype(xt.dtype))
    return y.reshape(B, S, D)


def trunk(x, ab_w_in, ab_conv_w, ab_conv_b, ab_conv_ln_g, ab_conv_ln_b, ab_sink, ab_w_out,
          c_w_in, c_q_norm, c_k_norm, c_w_out, ln1_g, ln1_b,
          moe_router, moe_w_gate, moe_w_up, moe_w_down, ln2_g, ln2_b):
    B, S, _ = x.shape
    pos = jnp.arange(S, dtype=jnp.float32)
    ang_b = rope_angles(pos, ROT_DIM_B, ROPE_THETA_B)
    rows = S // GRID_W
    grid_r, grid_c = jnp.meshgrid(jnp.arange(rows), jnp.arange(GRID_W), indexing='ij')
    ang_row = rope_angles(grid_r.reshape(-1).astype(jnp.float32), HEAD_DIM // 2, ROPE_THETA_C)
    ang_col = rope_angles(grid_c.reshape(-1).astype(jnp.float32), HEAD_DIM // 2, ROPE_THETA_C)
    for layer in range(DEPTH):
        i = layer // 2
        if layer % 2 == 0:
            mix = mixer_conv_window(x, ab_w_in[i], ab_conv_w[i], ab_conv_b[i], ab_conv_ln_g[i],
                                    ab_conv_ln_b[i], ab_sink[i], ab_w_out[i], ang_b)
        else:
            mix = mixer_axial(x, c_w_in[i], c_q_norm[i], c_k_norm[i], c_w_out[i], ang_row, ang_col)
        x = layer_norm(ALPHA * x + mix, ln1_g[layer], ln1_b[layer])
        ffn = expert_choice_ffn(x, moe_router[layer], moe_w_gate[layer], moe_w_up[layer], moe_w_down[layer])
        x = layer_norm(ALPHA * x + ffn, ln2_g[layer], ln2_b[layer])
    return x


def setup_inputs(seed: int = 0) -> dict:
    key = jax.random.key(seed)
    ks = jax.random.split(key, 24)
    nrm = jax.random.normal
    f32 = jnp.float32
    d = D_MODEL
    return {
        'x_prompt': nrm(ks[0], (BATCH, SEQ, d), f32),
        'x_sample': nrm(ks[1], (DEC_BATCH, DEC_SEQ, d), f32),
        'ab_w_in': nrm(ks[2], (N_EVEN, d, AB_IN), f32) * d ** -0.5,
        'ab_conv_w': nrm(ks[3], (N_EVEN, CONV_WIDTH, CONV_DIM), f32) * CONV_WIDTH ** -0.5,
        'ab_conv_b': nrm(ks[4], (N_EVEN, CONV_DIM), f32) * 0.02,
        'ab_conv_ln_g': 1.0 + 0.02 * nrm(ks[5], (N_EVEN, CONV_DIM), f32),
        'ab_conv_ln_b': 0.02 * nrm(ks[6], (N_EVEN, CONV_DIM), f32),
        'ab_sink': nrm(ks[7], (N_EVEN, WIN_HEADS), f32),
        'ab_w_out': nrm(ks[8], (N_EVEN, CONV_DIM + AB_Q, d), f32) * (CONV_DIM + AB_Q) ** -0.5 * BETA,
        'c_w_in': nrm(ks[9], (N_ODD, d, C_IN), f32) * d ** -0.5,
        'c_q_norm': 1.0 + 0.02 * nrm(ks[10], (N_ODD, HEAD_DIM), f32),
        'c_k_norm': 1.0 + 0.02 * nrm(ks[11], (N_ODD, HEAD_DIM), f32),
        'c_w_out': nrm(ks[12], (N_ODD, C_Q, d), f32) * C_Q ** -0.5 * BETA,
        'ln1_g': 1.0 + 0.02 * nrm(ks[13], (DEPTH, d), f32),
        'ln1_b': 0.02 * nrm(ks[14], (DEPTH, d), f32),
        'moe_router': nrm(ks[15], (DEPTH, d, N_EXPERTS), f32) * d ** -0.5,
        'moe_w_gate': nrm(ks[16], (DEPTH, N_EXPERTS, d, D_EXPERT), f32) * d ** -0.5,
        'moe_w_up': nrm(ks[17], (DEPTH, N_EXPERTS, d, D_EXPERT), f32) * d ** -0.5,
        'moe_w_down': nrm(ks[18], (DEPTH, N_EXPERTS, D_EXPERT, d), f32) * D_EXPERT ** -0.5 * BETA,
        'ln2_g': 1.0 + 0.02 * nrm(ks[19], (DEPTH, d), f32),
        'ln2_b': 0.02 * nrm(ks[20], (DEPTH, d), f32),
    }


def reference(x_prompt, x_sample, ab_w_in, ab_conv_w, ab_conv_b, ab_conv_ln_g, ab_conv_ln_b, ab_sink,
              ab_w_out, c_w_in, c_q_norm, c_k_norm, c_w_out, ln1_g, ln1_b,
              moe_router, moe_w_gate, moe_w_up, moe_w_down, ln2_g, ln2_b):
    y_prompt = trunk(x_prompt, ab_w_in, ab_conv_w, ab_conv_b, ab_conv_ln_g, ab_conv_ln_b, ab_sink,
                     ab_w_out, c_w_in, c_q_norm, c_k_norm, c_w_out, ln1_g, ln1_b,
                     moe_router, moe_w_gate, moe_w_up, moe_w_down, ln2_g, ln2_b)
    y_sample = trunk(x_sample, ab_w_in, ab_conv_w, ab_conv_b, ab_conv_ln_g, ab_conv_ln_b, ab_sink,
                     ab_w_out, c_w_in, c_q_norm, c_k_norm, c_w_out, ln1_g, ln1_b,
                     moe_router, moe_w_gate, moe_w_up, moe_w_down, ln2_g, ln2_b)
    return (y_prompt, y_sample)
```

```python
import functools

import jax
import jax.numpy as jnp
import numpy as np
from jax import lax
from jax.experimental import pallas as pl
from jax.experimental.pallas import tpu as pltpu

D_MODEL = 1024
DEPTH = 2
GRID_W = 64
BLOCK = 128
HEAD_DIM = 64
CONV_DIM = 512
CONV_WIDTH = 31
WIN_HEADS = 8
WIN_KV_HEADS = 2
WINDOW = 128
ROT_DIM_B = 16
ROPE_THETA_B = 500000.0
AX_HEADS = 16
AX_KV_HEADS = 4
ROPE_THETA_C = 10000.0
N_EXPERTS = 16
CAPACITY_FACTOR = 2
ALPHA = (2 * DEPTH) ** 0.25
AB_Q = WIN_HEADS * HEAD_DIM
AB_KV = WIN_KV_HEADS * HEAD_DIM
AB_IN = 2 * CONV_DIM + AB_Q + 2 * AB_KV
C_Q = AX_HEADS * HEAD_DIM
C_KV = AX_KV_HEADS * HEAD_DIM
C_IN = C_Q + 2 * C_KV
LN_EPS = 1e-5
RMS_EPS = 1e-6

LANES = 128
CONV_HALO = 16
VMEM_LIMIT = 48 * 1024 * 1024

BF16 = jnp.bfloat16
F32 = jnp.float32


def _cparams(sem):
    return pltpu.CompilerParams(dimension_semantics=sem, vmem_limit_bytes=VMEM_LIMIT)


def _row_tile(n, pref):
    t = min(n, pref)
    assert n % t == 0
    return t


def _rope_tables(cos, sin, lo, shift, scale):
    S = cos.shape[0]
    c = jnp.ones((S, HEAD_DIM), F32)
    sa = jnp.zeros((S, HEAD_DIM), F32)
    sb = jnp.zeros((S, HEAD_DIM), F32)
    c = c.at[:, lo:lo + shift].set(cos).at[:, lo + shift:lo + 2 * shift].set(cos)
    sa = sa.at[:, lo:lo + shift].set(-sin)
    sb = sb.at[:, lo + shift:lo + 2 * shift].set(sin)
    return c * scale, sa * scale, sb * scale


def _angles(pos, dim, theta):
    inv = theta ** (-jnp.arange(0, dim, 2, dtype=F32) / dim)
    return pos[:, None] * inv[None, :]


def _tables_window(S):
    ang = _angles(jnp.arange(S, dtype=F32), ROT_DIM_B, ROPE_THETA_B)
    cos, sin = jnp.cos(ang), jnp.sin(ang)
    out = []
    for scale in (HEAD_DIM ** -0.5, 1.0):
        c, sa, sb = _rope_tables(cos, sin, 0, ROT_DIM_B // 2, scale)
        out += [c, sa, sb]
    t = jnp.stack(out)
    return jnp.concatenate([t, t], axis=-1)


def _tables_axial(S):
    rows = S // GRID_W
    gr, gc = jnp.meshgrid(jnp.arange(rows), jnp.arange(GRID_W), indexing='ij')
    ang_r = _angles(gr.reshape(-1).astype(F32), HEAD_DIM // 2, ROPE_THETA_C)
    ang_c = _angles(gc.reshape(-1).astype(F32), HEAD_DIM // 2, ROPE_THETA_C)
    out = []
    for scale in (HEAD_DIM ** -0.5, 1.0):
        c1, sa1, sb1 = _rope_tables(jnp.cos(ang_r), jnp.sin(ang_r), 0, HEAD_DIM // 4, scale)
        c2, sa2, sb2 = _rope_tables(jnp.cos(ang_c), jnp.sin(ang_c), HEAD_DIM // 2, HEAD_DIM // 4, scale)
        half = jnp.arange(HEAD_DIM) < HEAD_DIM // 2
        out += [jnp.where(half, c1, c2), sa1 + sa2, sb1 + sb2]
    t = jnp.stack(out)
    return jnp.concatenate([t, t], axis=-1)


def _rope_chunk(x, c, sa, sb, shift):
    fwd = pltpu.roll(x, LANES - shift, axis=1)
    bwd = pltpu.roll(x, shift, axis=1)
    return x * c + fwd * sa + bwd * sb


def _inproj0_kernel(x_ref, w_ref, tab_ref, ug_ref, q_ref, k_ref, v_ref):
    h = jnp.dot(x_ref[...].astype(BF16), w_ref[...], preferred_element_type=F32)
    ug_ref[...] = h[:, :2 * CONV_DIM].astype(BF16)
    sh = ROT_DIM_B // 2
    q0 = 2 * CONV_DIM
    for c in range(AB_Q // LANES):
        xc = h[:, q0 + c * LANES:q0 + (c + 1) * LANES]
        q_ref[:, c * LANES:(c + 1) * LANES] = _rope_chunk(
            xc, tab_ref[0], tab_ref[1], tab_ref[2], sh).astype(BF16)
    k0 = q0 + AB_Q
    for c in range(AB_KV // LANES):
        xc = h[:, k0 + c * LANES:k0 + (c + 1) * LANES]
        k_ref[:, c * LANES:(c + 1) * LANES] = _rope_chunk(
            xc, tab_ref[3], tab_ref[4], tab_ref[5], sh).astype(BF16)
    v_ref[...] = h[:, k0 + AB_KV:].astype(BF16)


def _inproj0(x, w, tab, S):
    n = x.shape[0]
    tm = _row_tile(S, 512)
    nt = S // tm
    return pl.pallas_call(
        _inproj0_kernel,
        grid=(n // tm,),
        in_specs=[
            pl.BlockSpec((tm, D_MODEL), lambda i: (i, 0)),
            pl.BlockSpec((D_MODEL, AB_IN), lambda i: (0, 0)),
            pl.BlockSpec((6, tm, LANES), lambda i: (0, i % nt, 0)),
        ],
        out_specs=[
            pl.BlockSpec((tm, 2 * CONV_DIM), lambda i: (i, 0)),
            pl.BlockSpec((tm, AB_Q), lambda i: (i, 0)),
            pl.BlockSpec((tm, AB_KV), lambda i: (i, 0)),
            pl.BlockSpec((tm, AB_KV), lambda i: (i, 0)),
        ],
        out_shape=[
            jax.ShapeDtypeStruct((n, 2 * CONV_DIM), BF16),
            jax.ShapeDtypeStruct((n, AB_Q), BF16),
            jax.ShapeDtypeStruct((n, AB_KV), BF16),
            jax.ShapeDtypeStruct((n, AB_KV), BF16),
        ],
        compiler_params=_cparams(("parallel",)),
        name="inproj0",
    )(x, w, tab)


def _conv_kernel(nt, cur_ref, prev_ref, next_ref, w_ref, b_ref, g_ref, beta_ref, o_ref, buf_ref):
    T = cur_ref.shape[0]
    si = pl.program_id(0) % nt

    def glu(blk):
        u = blk[:, :CONV_DIM].astype(F32)
        gt = blk[:, CONV_DIM:].astype(F32)
        return u * jax.nn.sigmoid(gt)

    pv = jnp.where(si > 0, 1.0, 0.0).astype(F32)
    nv = jnp.where(si < nt - 1, 1.0, 0.0).astype(F32)
    buf_ref[0:CONV_HALO, :] = glu(prev_ref[...]) * pv
    buf_ref[CONV_HALO:CONV_HALO + T, :] = glu(cur_ref[...])
    buf_ref[CONV_HALO + T:, :] = glu(next_ref[...]) * nv

    off = CONV_HALO - CONV_WIDTH // 2
    acc = jnp.zeros((T, CONV_DIM), F32)
    for j in range(CONV_WIDTH):
        acc = acc + buf_ref[off + j:off + j + T, :] * w_ref[j:j + 1, :]
    hcv = acc + b_ref[...]
    mu = jnp.mean(hcv, axis=-1, keepdims=True)
    xc = hcv - mu
    var = jnp.mean(xc * xc, axis=-1, keepdims=True)
    y = xc * lax.rsqrt(var + LN_EPS) * g_ref[...] + beta_ref[...]
    o_ref[...] = (y * jax.nn.sigmoid(y)).astype(BF16)


def _conv_branch(ug, w, b, g, beta, S):
    n = ug.shape[0]
    T = _row_tile(S, 512)
    nt = S // T
    r = T // CONV_HALO
    nh = n // CONV_HALO
    vec = lambda: pl.BlockSpec((1, CONV_DIM), lambda i: (0, 0))
    return pl.pallas_call(
        functools.partial(_conv_kernel, nt),
        grid=(n // T,),
        in_specs=[
            pl.BlockSpec((T, 2 * CONV_DIM), lambda i: (i, 0)),
            pl.BlockSpec((CONV_HALO, 2 * CONV_DIM), lambda i: (jnp.maximum(i * r - 1, 0), 0)),
            pl.BlockSpec((CONV_HALO, 2 * CONV_DIM), lambda i: (jnp.minimum((i + 1) * r, nh - 1), 0)),
            pl.BlockSpec((CONV_WIDTH, CONV_DIM), lambda i: (0, 0)),
            vec(), vec(), vec(),
        ],
        out_specs=pl.BlockSpec((T, CONV_DIM), lambda i: (i, 0)),
        out_shape=jax.ShapeDtypeStruct((n, CONV_DIM), BF16),
        scratch_shapes=[pltpu.VMEM((T + 2 * CONV_HALO, CONV_DIM), F32)],
        compiler_params=_cparams(("parallel",)),
        name="conv_branch",
    )(ug, ug, ug, w, b.reshape(1, -1), g.reshape(1, -1), beta.reshape(1, -1))


def _winattn_kernel(nb, sink_ref, q_ref, kp_ref, kc_ref, kn_ref, vp_ref, vc_ref, vn_ref, o_ref):
    qi = pl.program_id(1)
    row = lax.broadcasted_iota(jnp.int32, (BLOCK, 3 * BLOCK), 0)
    col = lax.broadcasted_iota(jnp.int32, (BLOCK, 3 * BLOCK), 1)
    valid = jnp.abs(col - BLOCK - row) <= WINDOW
    valid = valid & ((qi > 0) | (col >= BLOCK)) & ((qi < nb - 1) | (col < 2 * BLOCK))
    G = WIN_HEADS // WIN_KV_HEADS
    for g in range(WIN_KV_HEADS):
        sl = slice(g * HEAD_DIM, (g + 1) * HEAD_DIM)
        kk = jnp.concatenate([kp_ref[:, sl], kc_ref[:, sl], kn_ref[:, sl]], axis=0)
        vv = jnp.concatenate([vp_ref[:, sl], vc_ref[:, sl], vn_ref[:, sl]], axis=0)
        for j in range(G):
            h = g * G + j
            qh = q_ref[:, h * HEAD_DIM:(h + 1) * HEAD_DIM]
            s = lax.dot_general(qh, kk, (((1,), (1,)), ((), ())), preferred_element_type=F32)
            s = jnp.where(valid, s, -jnp.inf)
            sk = sink_ref[h]
            m = jnp.maximum(jnp.max(s, axis=-1, keepdims=True), sk)
            p = jnp.exp(s - m)
            den = jnp.sum(p, axis=-1, keepdims=True) + jnp.exp(sk - m)
            o = jnp.dot(p.astype(BF16), vv, preferred_element_type=F32) / den
            o_ref[:, h * HEAD_DIM:(h + 1) * HEAD_DIM] = o.astype(BF16)


def _window_attention(q, k, v, sink, B, S):
    n = q.shape[0]
    nb = S // BLOCK
    cur = lambda b, i: (b * nb + i, 0)
    prev = lambda b, i: (b * nb + jnp.maximum(i - 1, 0), 0)
    nxt = lambda b, i: (b * nb + jnp.minimum(i + 1, nb - 1), 0)
    kv = lambda f: pl.BlockSpec((BLOCK, AB_KV), f)
    return pl.pallas_call(
        functools.partial(_winattn_kernel, nb),
        grid=(B, nb),
        in_specs=[
            pl.BlockSpec(memory_space=pltpu.SMEM),
            pl.BlockSpec((BLOCK, AB_Q), cur),
            kv(prev), kv(cur), kv(nxt), kv(prev), kv(cur), kv(nxt),
        ],
        out_specs=pl.BlockSpec((BLOCK, AB_Q), cur),
        out_shape=jax.ShapeDtypeStruct((n, AB_Q), BF16),
        compiler_params=_cparams(("parallel", "parallel")),
        name="window_attention",
    )(sink, q, k, k, k, v, v, v)


def _split_dot(x, w_hi, w_lo):
    x_hi = x.astype(BF16)
    x_lo = (x - x_hi.astype(F32)).astype(BF16)
    out = jnp.dot(x_hi, w_hi, preferred_element_type=F32)
    out = out + jnp.dot(x_lo, w_hi, preferred_element_type=F32)
    return out + jnp.dot(x_hi, w_lo, preferred_element_type=F32)


def _outproj_kernel(n_act, *refs):
    acts = refs[:n_act]
    x_ref, w_ref, g_ref, b_ref, rhi_ref, rlo_ref, xo_ref, xb_ref, aff_ref = refs[n_act:]
    a = acts[0][...] if n_act == 1 else jnp.concatenate([r[...] for r in acts], axis=-1)
    mix = jnp.dot(a, w_ref[...], preferred_element_type=F32)
    y = ALPHA * x_ref[...] + mix
    mu = jnp.mean(y, axis=-1, keepdims=True)
    yc = y - mu
    var = jnp.mean(yc * yc, axis=-1, keepdims=True)
    x1 = yc * lax.rsqrt(var + LN_EPS) * g_ref[...] + b_ref[...]
    xo_ref[...] = x1
    xb_ref[...] = x1.astype(BF16)
    logits = _split_dot(x1, rhi_ref[...], rlo_ref[...])
    m = jnp.max(logits, axis=-1, keepdims=True)
    e = jnp.exp(logits - m)
    aff_ref[...] = e / jnp.sum(e, axis=-1, keepdims=True)


def _outproj_ln_router(acts, x, w, g, b, w_r):
    n = x.shape[0]
    tm = _row_tile(n, 512)
    r_hi = w_r.astype(BF16)
    r_lo = (w_r - r_hi.astype(F32)).astype(BF16)
    row = lambda width: pl.BlockSpec((tm, width), lambda i: (i, 0))
    const = lambda shape: pl.BlockSpec(shape, lambda i: (0, 0))
    return pl.pallas_call(
        functools.partial(_outproj_kernel, len(acts)),
        grid=(n // tm,),
        in_specs=[row(a.shape[1]) for a in acts] + [
            row(D_MODEL), const((D_MODEL, D_MODEL)), const((1, D_MODEL)), const((1, D_MODEL)),
            const((D_MODEL, N_EXPERTS)), const((D_MODEL, N_EXPERTS)),
        ],
        out_specs=[row(D_MODEL), row(D_MODEL), row(N_EXPERTS)],
        out_shape=[
            jax.ShapeDtypeStruct((n, D_MODEL), F32),
            jax.ShapeDtypeStruct((n, D_MODEL), BF16),
            jax.ShapeDtypeStruct((n, N_EXPERTS), F32),
        ],
        compiler_params=_cparams(("parallel",)),
        name="outproj_ln_router",
    )(*acts, x, w, g.reshape(1, -1), b.reshape(1, -1), r_hi, r_lo)


def _topk_kernel(cap, aff_ref, gate_ref, mask_ref):
    n = aff_ref.shape[1]
    bits = pltpu.bitcast(aff_ref[...], jnp.int32)
    count = lambda m: jnp.sum(m.astype(jnp.int32), axis=1, keepdims=True)

    def value_step(i, t):
        cand = t | (jnp.int32(1) << (30 - i))
        return jnp.where(count(bits >= cand) >= cap, cand, t)

    thr = lax.fori_loop(0, 31, value_step, jnp.zeros((N_EXPERTS, 1), jnp.int32))
    gt = bits > thr
    eq = bits == thr
    need = cap - count(gt)
    idx = lax.broadcasted_iota(jnp.int32, (N_EXPERTS, n), 1)
    nbits = max(int(n).bit_length(), 1)

    def index_step(i, j0):
        cand = j0 | (jnp.int32(1) << (nbits - 1 - i))
        return jnp.where(count(eq & (idx < cand)) < need, cand, j0)

    j0 = lax.fori_loop(0, nbits, index_step, jnp.zeros((N_EXPERTS, 1), jnp.int32))
    sel = gt | (eq & (idx <= j0))
    gate_ref[...] = jnp.where(sel, aff_ref[...], 0.0)
    mask_ref[...] = sel.astype(jnp.int32)


def _expert_choice(aff_t, cap):
    n = aff_t.shape[1]
    return pl.pallas_call(
        functools.partial(_topk_kernel, cap),
        out_shape=[
            jax.ShapeDtypeStruct((N_EXPERTS, n), F32),
            jax.ShapeDtypeStruct((N_EXPERTS, n), jnp.int32),
        ],
        compiler_params=pltpu.CompilerParams(vmem_limit_bytes=VMEM_LIMIT),
        name="expert_choice_topk",
    )(aff_t)


def _ffn_kernel(x_ref, g_ref, wg_ref, wu_ref, wd_ref, o_ref):
    x = x_ref[...]
    hg = jnp.dot(x, wg_ref[0], preferred_element_type=F32)
    hu = jnp.dot(x, wu_ref[0], preferred_element_type=F32)
    h = (hg * jax.nn.sigmoid(hg) * hu).astype(BF16)
    y = jnp.dot(h, wd_ref[0], preferred_element_type=F32)
    o_ref[...] = y * g_ref[:, 0:1]


def _expert_ffn(xe, gate, wg, wu, wd, cap):
    rows = xe.shape[0]
    tm = _row_tile(cap, 512)
    nr = cap // tm
    wspec = pl.BlockSpec((1, D_MODEL, D_MODEL), lambda e, r: (e, 0, 0))
    return pl.pallas_call(
        _ffn_kernel,
        grid=(N_EXPERTS, nr),
        in_specs=[
            pl.BlockSpec((tm, D_MODEL), lambda e, r: (e * nr + r, 0)),
            pl.BlockSpec((tm, LANES), lambda e, r: (e * nr + r, 0)),
            wspec, wspec, wspec,
        ],
        out_specs=pl.BlockSpec((tm, D_MODEL), lambda e, r: (e * nr + r, 0)),
        out_shape=jax.ShapeDtypeStruct((rows, D_MODEL), F32),
        compiler_params=_cparams(("parallel", "arbitrary")),
        name="expert_ffn",
    )(xe, gate, wg, wu, wd)


def _add_ln_kernel(x_ref, y_ref, g_ref, b_ref, o_ref):
    y = ALPHA * x_ref[...] + y_ref[...]
    mu = jnp.mean(y, axis=-1, keepdims=True)
    yc = y - mu
    var = jnp.mean(yc * yc, axis=-1, keepdims=True)
    o_ref[...] = yc * lax.rsqrt(var + LN_EPS) * g_ref[...] + b_ref[...]


def _add_ln(x, y, g, b):
    n = x.shape[0]
    tm = _row_tile(n, 512)
    row = pl.BlockSpec((tm, D_MODEL), lambda i: (i, 0))
    vec = pl.BlockSpec((1, D_MODEL), lambda i: (0, 0))
    return pl.pallas_call(
        _add_ln_kernel,
        grid=(n // tm,),
        in_specs=[row, row, vec, vec],
        out_specs=row,
        out_shape=jax.ShapeDtypeStruct((n, D_MODEL), F32),
        compiler_params=_cparams(("parallel",)),
        name="add_ln",
    )(x, y, g.reshape(1, -1), b.reshape(1, -1))


def _inproj1_kernel(x_ref, w_ref, tab_ref, gq_ref, gk_ref, seg_ref, q_ref, k_ref, v_ref):
    h = jnp.dot(x_ref[...].astype(BF16), w_ref[...], preferred_element_type=F32)
    sh = HEAD_DIM // 4
    seg = seg_ref[...]

    def norm_rope(xc, gain, t0):
        ms = jnp.dot((xc * xc).astype(BF16), seg, preferred_element_type=F32)
        xn = xc * lax.rsqrt(ms + RMS_EPS) * gain
        return _rope_chunk(xn, tab_ref[t0], tab_ref[t0 + 1], tab_ref[t0 + 2], sh)

    for c in range(C_Q // LANES):
        xc = h[:, c * LANES:(c + 1) * LANES]
        q_ref[:, c * LANES:(c + 1) * LANES] = norm_rope(xc, gq_ref[...], 0).astype(BF16)
    for c in range(C_KV // LANES):
        xc = h[:, C_Q + c * LANES:C_Q + (c + 1) * LANES]
        kc = norm_rope(xc, gk_ref[...], 3).astype(BF16)
        k_ref[2 * c] = kc[:, :HEAD_DIM]
        k_ref[2 * c + 1] = kc[:, HEAD_DIM:]
        vc = h[:, C_Q + C_KV + c * LANES:C_Q + C_KV + (c + 1) * LANES].astype(BF16)
        v_ref[2 * c] = vc[:, :HEAD_DIM]
        v_ref[2 * c + 1] = vc[:, HEAD_DIM:]


def _inproj1(x, w, tab, gq, gk, S):
    n = x.shape[0]
    tm = _row_tile(S, 512)
    nt = S // tm
    lane_head = np.arange(LANES) // HEAD_DIM
    seg = jnp.asarray((lane_head[:, None] == lane_head[None, :]) / HEAD_DIM, BF16)
    tile2 = lambda g: jnp.concatenate([g, g]).reshape(1, LANES)
    const = lambda shape: pl.BlockSpec(shape, lambda i: (0,) * len(shape))
    return pl.pallas_call(
        _inproj1_kernel,
        grid=(n // tm,),
        in_specs=[
            pl.BlockSpec((tm, D_MODEL), lambda i: (i, 0)),
            const((D_MODEL, C_IN)),
            pl.BlockSpec((6, tm, LANES), lambda i: (0, i % nt, 0)),
            const((1, LANES)), const((1, LANES)), const((LANES, LANES)),
        ],
        out_specs=[
            pl.BlockSpec((tm, C_Q), lambda i: (i, 0)),
            pl.BlockSpec((AX_KV_HEADS, tm, HEAD_DIM), lambda i: (0, i, 0)),
            pl.BlockSpec((AX_KV_HEADS, tm, HEAD_DIM), lambda i: (0, i, 0)),
        ],
        out_shape=[
            jax.ShapeDtypeStruct((n, C_Q), BF16),
            jax.ShapeDtypeStruct((AX_KV_HEADS, n, HEAD_DIM), BF16),
            jax.ShapeDtypeStruct((AX_KV_HEADS, n, HEAD_DIM), BF16),
        ],
        compiler_params=_cparams(("parallel",)),
        name="inproj1",
    )(x, w, tab, tile2(gq), tile2(gk), seg)


def _axattn_kernel(q_ref, k_ref, v_ref, o_ref):
    G = AX_HEADS // AX_KV_HEADS
    TQ = q_ref.shape[0]
    q = q_ref[...]
    qs = jnp.concatenate([q[:, j * HEAD_DIM:(j + 1) * HEAD_DIM] for j in range(G)], axis=0)
    s = lax.dot_general(qs, k_ref[...], (((1,), (1,)), ((), ())), preferred_element_type=F32)
    m = jnp.max(s, axis=-1, keepdims=True)
    p = jnp.exp(s - m)
    den = jnp.sum(p, axis=-1, keepdims=True)
    o = jnp.dot(p.astype(BF16), v_ref[...], preferred_element_type=F32) / den
    for j in range(G):
        o_ref[:, j * HEAD_DIM:(j + 1) * HEAD_DIM] = o[j * TQ:(j + 1) * TQ].astype(BF16)


def _axial_attention(q, k, v, B, S):
    n = q.shape[0]
    TQ = BLOCK
    nq = S // TQ
    G = AX_HEADS // AX_KV_HEADS
    qspec = pl.BlockSpec((TQ, G * HEAD_DIM), lambda b, g, i: (b * nq + i, g))
    kvspec = pl.BlockSpec((None, None, S, HEAD_DIM), lambda b, g, i: (g, b, 0, 0))
    return pl.pallas_call(
        _axattn_kernel,
        grid=(B, AX_KV_HEADS, nq),
        in_specs=[qspec, kvspec, kvspec],
        out_specs=qspec,
        out_shape=jax.ShapeDtypeStruct((n, C_Q), BF16),
        compiler_params=_cparams(("parallel", "parallel", "arbitrary")),
        name="axial_attention",
    )(q, k, v)


def _moe(x1, xb, aff, wg, wu, wd, n):
    cap = CAPACITY_FACTOR * n // N_EXPERTS
    gate_t, mask_t = _expert_choice(aff.T, cap)
    pos = jnp.cumsum(mask_t, axis=1) - 1
    tok = jnp.broadcast_to(jnp.arange(n, dtype=jnp.int32), (N_EXPERTS, n))
    slot = jnp.where(mask_t > 0, pos, cap)
    e_ix = jnp.broadcast_to(jnp.arange(N_EXPERTS)[:, None], (N_EXPERTS, n))
    idx = jnp.zeros((N_EXPERTS, cap), jnp.int32).at[e_ix, slot].set(tok, mode='drop')
    gsel = jnp.take_along_axis(gate_t, idx, axis=1).reshape(-1)
    idx = idx.reshape(-1)
    xe = jnp.take(xb, idx, axis=0)
    gb = jnp.broadcast_to(gsel[:, None], (gsel.shape[0], LANES))
    ye = _expert_ffn(xe, gb, wg, wu, wd, cap)
    return jnp.zeros_like(x1).at[idx].add(ye)


def _trunk(x, p):
    B, S, _ = x.shape
    n = B * S
    xf = x.reshape(n, D_MODEL)
    tab_w = _tables_window(S)
    tab_a = _tables_axial(S)
    for layer in range(DEPTH):
        i = layer // 2
        if layer % 2 == 0:
            ug, q, k, v = _inproj0(xf, p['ab_w_in'][i], tab_w, S)
            conv = _conv_branch(ug, p['ab_conv_w'][i], p['ab_conv_b'][i], p['ab_conv_ln_g'][i],
                                p['ab_conv_ln_b'][i], S)
            att = _window_attention(q, k, v, p['ab_sink'][i], B, S)
            acts, w_out = [conv, att], p['ab_w_out'][i]
        else:
            q, k, v = _inproj1(xf, p['c_w_in'][i], tab_a, p['c_q_norm'][i], p['c_k_norm'][i], S)
            k = k.reshape(AX_KV_HEADS, B, S, HEAD_DIM)
            v = v.reshape(AX_KV_HEADS, B, S, HEAD_DIM)
            acts, w_out = [_axial_attention(q, k, v, B, S)], p['c_w_out'][i]
        x1, xb, aff = _outproj_ln_router(acts, xf, w_out, p['ln1_g'][layer], p['ln1_b'][layer],
                                         p['moe_router'][layer])
        y = _moe(x1, xb, aff, p['moe_w_gate'][layer], p['moe_w_up'][layer], p['moe_w_down'][layer], n)
        xf = _add_ln(x1, y, p['ln2_g'][layer], p['ln2_b'][layer])
    return xf.reshape(B, S, D_MODEL)


def kernel(x_prompt, x_sample, ab_w_in, ab_conv_w, ab_conv_b, ab_conv_ln_g, ab_conv_ln_b, ab_sink, ab_w_out, c_w_in, c_q_norm, c_k_norm, c_w_out, ln1_g, ln1_b, moe_router, moe_w_gate, moe_w_up, moe_w_down, ln2_g, ln2_b):
    p = dict(
        ab_w_in=ab_w_in.astype(BF16), ab_conv_w=ab_conv_w, ab_conv_b=ab_conv_b,
        ab_conv_ln_g=ab_conv_ln_g, ab_conv_ln_b=ab_conv_ln_b, ab_sink=ab_sink,
        ab_w_out=ab_w_out.astype(BF16), c_w_in=c_w_in.astype(BF16), c_q_norm=c_q_norm,
        c_k_norm=c_k_norm, c_w_out=c_w_out.astype(BF16), ln1_g=ln1_g, ln1_b=ln1_b,
        moe_router=moe_router, moe_w_gate=moe_w_gate.astype(BF16), moe_w_up=moe_w_up.astype(BF16),
        moe_w_down=moe_w_down.astype(BF16), ln2_g=ln2_g, ln2_b=ln2_b,
    )
    return (_trunk(x_prompt, p), _trunk(x_sample, p))
```

```python
import functools

import jax
import jax.numpy as jnp
import numpy as np
from jax import lax
from jax.experimental import pallas as pl
from jax.experimental.pallas import tpu as pltpu
from jax.experimental.pallas import tpu_sc as plsc

D_MODEL = 1024
DEPTH = 2
GRID_W = 64
BLOCK = 128
HEAD_DIM = 64
CONV_DIM = 512
CONV_WIDTH = 31
WIN_HEADS = 8
WIN_KV_HEADS = 2
WINDOW = 128
ROT_DIM_B = 16
ROPE_THETA_B = 500000.0
AX_HEADS = 16
AX_KV_HEADS = 4
ROPE_THETA_C = 10000.0
N_EXPERTS = 16
CAPACITY_FACTOR = 2
ALPHA = (2 * DEPTH) ** 0.25
AB_Q = WIN_HEADS * HEAD_DIM
AB_KV = WIN_KV_HEADS * HEAD_DIM
AB_IN = 2 * CONV_DIM + AB_Q + 2 * AB_KV
C_Q = AX_HEADS * HEAD_DIM
C_KV = AX_KV_HEADS * HEAD_DIM
C_IN = C_Q + 2 * C_KV
LN_EPS = 1e-5
RMS_EPS = 1e-6

LANES = 128
CONV_HALO = 16
VMEM_LIMIT = 48 * 1024 * 1024

TOKEN_TILE = 256
Z_CHUNK = 256
ROW_ALIGN = 16
SC_WINDOW = 64
COMPACT_BLOCK = 256
AX_Q_TILE = 256

BF16 = jnp.bfloat16
F32 = jnp.float32
U32 = jnp.uint32
PACKED = D_MODEL // 2


def _pack_rows(x):
    b = lax.bitcast_convert_type(x.astype(BF16).astype(F32), U32)
    return (b[:, :PACKED] >> 16) | (b[:, PACKED:] & jnp.uint32(0xFFFF0000))


def _unpack_rows(w):
    lo = lax.bitcast_convert_type(w << 16, F32)
    hi = lax.bitcast_convert_type(w & jnp.uint32(0xFFFF0000), F32)
    return jnp.concatenate([lo, hi], axis=1).astype(BF16)


def _cparams(sem):
    return pltpu.CompilerParams(dimension_semantics=sem, vmem_limit_bytes=VMEM_LIMIT)


def _row_tile(n, pref):
    t = min(n, pref)
    assert n % t == 0
    return t


def _rope_tables(cos, sin, lo, shift, scale):
    S = cos.shape[0]
    c = jnp.ones((S, HEAD_DIM), F32)
    sa = jnp.zeros((S, HEAD_DIM), F32)
    sb = jnp.zeros((S, HEAD_DIM), F32)
    c = c.at[:, lo:lo + shift].set(cos).at[:, lo + shift:lo + 2 * shift].set(cos)
    sa = sa.at[:, lo:lo + shift].set(-sin)
    sb = sb.at[:, lo + shift:lo + 2 * shift].set(sin)
    return c * scale, sa * scale, sb * scale


def _angles(pos, dim, theta):
    inv = theta ** (-jnp.arange(0, dim, 2, dtype=F32) / dim)
    return pos[:, None] * inv[None, :]


def _tables_window(S):
    ang = _angles(jnp.arange(S, dtype=F32), ROT_DIM_B, ROPE_THETA_B)
    cos, sin = jnp.cos(ang), jnp.sin(ang)
    out = []
    for scale in (HEAD_DIM ** -0.5, 1.0):
        c, sa, sb = _rope_tables(cos, sin, 0, ROT_DIM_B // 2, scale)
        out += [c, sa, sb]
    t = jnp.stack(out)
    return jnp.concatenate([t, t], axis=-1)


def _tables_axial(S):
    rows = S // GRID_W
    gr, gc = jnp.meshgrid(jnp.arange(rows), jnp.arange(GRID_W), indexing='ij')
    ang_r = _angles(gr.reshape(-1).astype(F32), HEAD_DIM // 2, ROPE_THETA_C)
    ang_c = _angles(gc.reshape(-1).astype(F32), HEAD_DIM // 2, ROPE_THETA_C)
    out = []
    for scale in (HEAD_DIM ** -0.5, 1.0):
        c1, sa1, sb1 = _rope_tables(jnp.cos(ang_r), jnp.sin(ang_r), 0, HEAD_DIM // 4, scale)
        c2, sa2, sb2 = _rope_tables(jnp.cos(ang_c), jnp.sin(ang_c), HEAD_DIM // 2, HEAD_DIM // 4, scale)
        half = jnp.arange(HEAD_DIM) < HEAD_DIM // 2
        out += [jnp.where(half, c1, c2), sa1 + sa2, sb1 + sb2]
    t = jnp.stack(out)
    return jnp.concatenate([t, t], axis=-1)


def _rope_chunk(x, c, sa, sb, shift):
    fwd = pltpu.roll(x, LANES - shift, axis=1)
    bwd = pltpu.roll(x, shift, axis=1)
    return x * c + fwd * sa + bwd * sb


def _inproj0_kernel(x_ref, w_ref, tab_ref, ug_ref, q_ref, k_ref, v_ref):
    h = jnp.dot(x_ref[...].astype(BF16), w_ref[...], preferred_element_type=F32)
    ug_ref[...] = h[:, :2 * CONV_DIM].astype(BF16)
    sh = ROT_DIM_B // 2
    q0 = 2 * CONV_DIM
    for c in range(AB_Q // LANES):
        xc = h[:, q0 + c * LANES:q0 + (c + 1) * LANES]
        q_ref[:, c * LANES:(c + 1) * LANES] = _rope_chunk(
            xc, tab_ref[0], tab_ref[1], tab_ref[2], sh).astype(BF16)
    k0 = q0 + AB_Q
    for c in range(AB_KV // LANES):
        xc = h[:, k0 + c * LANES:k0 + (c + 1) * LANES]
        k_ref[:, c * LANES:(c + 1) * LANES] = _rope_chunk(
            xc, tab_ref[3], tab_ref[4], tab_ref[5], sh).astype(BF16)
    v_ref[...] = h[:, k0 + AB_KV:].astype(BF16)


def _inproj0(x, w, tab, S):
    n = x.shape[0]
    tm = _row_tile(S, 512)
    nt = S // tm
    return pl.pallas_call(
        _inproj0_kernel,
        grid=(n // tm,),
        in_specs=[
            pl.BlockSpec((tm, D_MODEL), lambda i: (i, 0)),
            pl.BlockSpec((D_MODEL, AB_IN), lambda i: (0, 0)),
            pl.BlockSpec((6, tm, LANES), lambda i: (0, i % nt, 0)),
        ],
        out_specs=[
            pl.BlockSpec((tm, 2 * CONV_DIM), lambda i: (i, 0)),
            pl.BlockSpec((tm, AB_Q), lambda i: (i, 0)),
            pl.BlockSpec((tm, AB_KV), lambda i: (i, 0)),
            pl.BlockSpec((tm, AB_KV), lambda i: (i, 0)),
        ],
        out_shape=[
            jax.ShapeDtypeStruct((n, 2 * CONV_DIM), BF16),
            jax.ShapeDtypeStruct((n, AB_Q), BF16),
            jax.ShapeDtypeStruct((n, AB_KV), BF16),
            jax.ShapeDtypeStruct((n, AB_KV), BF16),
        ],
        compiler_params=_cparams(("parallel",)),
        name="inproj0",
    )(x, w, tab)


def _conv_kernel(nt, cur_ref, prev_ref, next_ref, w_ref, b_ref, g_ref, beta_ref, o_ref, buf_ref):
    T = cur_ref.shape[0]
    si = pl.program_id(0) % nt

    def glu(blk):
        u = blk[:, :CONV_DIM].astype(F32)
        gt = blk[:, CONV_DIM:].astype(F32)
        return u * jax.nn.sigmoid(gt)

    pv = jnp.where(si > 0, 1.0, 0.0).astype(F32)
    nv = jnp.where(si < nt - 1, 1.0, 0.0).astype(F32)
    buf_ref[0:CONV_HALO, :] = glu(prev_ref[...]) * pv
    buf_ref[CONV_HALO:CONV_HALO + T, :] = glu(cur_ref[...])
    buf_ref[CONV_HALO + T:, :] = glu(next_ref[...]) * nv

    off = CONV_HALO - CONV_WIDTH // 2
    acc = jnp.zeros((T, CONV_DIM), F32)
    for j in range(CONV_WIDTH):
        acc = acc + buf_ref[off + j:off + j + T, :] * w_ref[j:j + 1, :]
    hcv = acc + b_ref[...]
    mu = jnp.mean(hcv, axis=-1, keepdims=True)
    xc = hcv - mu
    var = jnp.mean(xc * xc, axis=-1, keepdims=True)
    y = xc * lax.rsqrt(var + LN_EPS) * g_ref[...] + beta_ref[...]
    o_ref[...] = (y * jax.nn.sigmoid(y)).astype(BF16)


def _conv_branch(ug, w, b, g, beta, S):
    n = ug.shape[0]
    T = _row_tile(S, 512)
    nt = S // T
    r = T // CONV_HALO
    nh = n // CONV_HALO
    vec = lambda: pl.BlockSpec((1, CONV_DIM), lambda i: (0, 0))
    return pl.pallas_call(
        functools.partial(_conv_kernel, nt),
        grid=(n // T,),
        in_specs=[
            pl.BlockSpec((T, 2 * CONV_DIM), lambda i: (i, 0)),
            pl.BlockSpec((CONV_HALO, 2 * CONV_DIM), lambda i: (jnp.maximum(i * r - 1, 0), 0)),
            pl.BlockSpec((CONV_HALO, 2 * CONV_DIM), lambda i: (jnp.minimum((i + 1) * r, nh - 1), 0)),
            pl.BlockSpec((CONV_WIDTH, CONV_DIM), lambda i: (0, 0)),
            vec(), vec(), vec(),
        ],
        out_specs=pl.BlockSpec((T, CONV_DIM), lambda i: (i, 0)),
        out_shape=jax.ShapeDtypeStruct((n, CONV_DIM), BF16),
        scratch_shapes=[pltpu.VMEM((T + 2 * CONV_HALO, CONV_DIM), F32)],
        compiler_params=_cparams(("parallel",)),
        name="conv_branch",
    )(ug, ug, ug, w, b.reshape(1, -1), g.reshape(1, -1), beta.reshape(1, -1))


def _winattn_kernel(nb, sink_ref, q_ref, kp_ref, kc_ref, kn_ref, vp_ref, vc_ref, vn_ref, o_ref):
    qi = pl.program_id(1)
    row = lax.broadcasted_iota(jnp.int32, (BLOCK, 3 * BLOCK), 0)
    col = lax.broadcasted_iota(jnp.int32, (BLOCK, 3 * BLOCK), 1)
    valid = jnp.abs(col - BLOCK - row) <= WINDOW
    valid = valid & ((qi > 0) | (col >= BLOCK)) & ((qi < nb - 1) | (col < 2 * BLOCK))
    G = WIN_HEADS // WIN_KV_HEADS
    for g in range(WIN_KV_HEADS):
        sl = slice(g * HEAD_DIM, (g + 1) * HEAD_DIM)
        kk = jnp.concatenate([kp_ref[:, sl], kc_ref[:, sl], kn_ref[:, sl]], axis=0)
        vv = jnp.concatenate([vp_ref[:, sl], vc_ref[:, sl], vn_ref[:, sl]], axis=0)
        for j in range(G):
            h = g * G + j
            qh = q_ref[:, h * HEAD_DIM:(h + 1) * HEAD_DIM]
            s = lax.dot_general(qh, kk, (((1,), (1,)), ((), ())), preferred_element_type=F32)
            s = jnp.where(valid, s, -jnp.inf)
            sk = sink_ref[h]
            m = jnp.maximum(jnp.max(s, axis=-1, keepdims=True), sk)
            p = jnp.exp(s - m)
            den = jnp.sum(p, axis=-1, keepdims=True) + jnp.exp(sk - m)
            o = jnp.dot(p.astype(BF16), vv, preferred_element_type=F32) / den
            o_ref[:, h * HEAD_DIM:(h + 1) * HEAD_DIM] = o.astype(BF16)


def _window_attention(q, k, v, sink, B, S):
    n = q.shape[0]
    nb = S // BLOCK
    cur = lambda b, i: (b * nb + i, 0)
    prev = lambda b, i: (b * nb + jnp.maximum(i - 1, 0), 0)
    nxt = lambda b, i: (b * nb + jnp.minimum(i + 1, nb - 1), 0)
    kv = lambda f: pl.BlockSpec((BLOCK, AB_KV), f)
    return pl.pallas_call(
        functools.partial(_winattn_kernel, nb),
        grid=(B, nb),
        in_specs=[
            pl.BlockSpec(memory_space=pltpu.SMEM),
            pl.BlockSpec((BLOCK, AB_Q), cur),
            kv(prev), kv(cur), kv(nxt), kv(prev), kv(cur), kv(nxt),
        ],
        out_specs=pl.BlockSpec((BLOCK, AB_Q), cur),
        out_shape=jax.ShapeDtypeStruct((n, AB_Q), BF16),
        compiler_params=_cparams(("parallel", "parallel")),
        name="window_attention",
    )(sink, q, k, k, k, v, v, v)


def _split_dot(x, w_hi, w_lo):
    x_hi = x.astype(BF16)
    x_lo = (x - x_hi.astype(F32)).astype(BF16)
    out = jnp.dot(x_hi, w_hi, preferred_element_type=F32)
    out = out + jnp.dot(x_lo, w_hi, preferred_element_type=F32)
    return out + jnp.dot(x_hi, w_lo, preferred_element_type=F32)


def _outproj_kernel(n_act, *refs):
    acts = refs[:n_act]
    x_ref, w_ref, g_ref, b_ref, rhi_ref, rlo_ref, xo_ref, xb_ref, aff_ref = refs[n_act:]
    a = acts[0][...] if n_act == 1 else jnp.concatenate([r[...] for r in acts], axis=-1)
    mix = jnp.dot(a, w_ref[...], preferred_element_type=F32)
    y = ALPHA * x_ref[...] + mix
    mu = jnp.mean(y, axis=-1, keepdims=True)
    yc = y - mu
    var = jnp.mean(yc * yc, axis=-1, keepdims=True)
    x1 = yc * lax.rsqrt(var + LN_EPS) * g_ref[...] + b_ref[...]
    xo_ref[...] = x1
    xb_ref[...] = _pack_rows(x1)
    logits = _split_dot(x1, rhi_ref[...], rlo_ref[...])
    m = jnp.max(logits, axis=-1, keepdims=True)
    e = jnp.exp(logits - m)
    aff_ref[...] = e / jnp.sum(e, axis=-1, keepdims=True)


def _outproj_ln_router(acts, x, w, g, b, w_r):
    n = x.shape[0]
    tm = _row_tile(n, 512)
    r_hi = w_r.astype(BF16)
    r_lo = (w_r - r_hi.astype(F32)).astype(BF16)
    row = lambda width: pl.BlockSpec((tm, width), lambda i: (i, 0))
    const = lambda shape: pl.BlockSpec(shape, lambda i: (0, 0))
    return pl.pallas_call(
        functools.partial(_outproj_kernel, len(acts)),
        grid=(n // tm,),
        in_specs=[row(a.shape[1]) for a in acts] + [
            row(D_MODEL), const((D_MODEL, D_MODEL)), const((1, D_MODEL)), const((1, D_MODEL)),
            const((D_MODEL, N_EXPERTS)), const((D_MODEL, N_EXPERTS)),
        ],
        out_specs=[row(D_MODEL), row(PACKED), row(N_EXPERTS)],
        out_shape=[
            jax.ShapeDtypeStruct((n, D_MODEL), F32),
            jax.ShapeDtypeStruct((n, PACKED), U32),
            jax.ShapeDtypeStruct((n, N_EXPERTS), F32),
        ],
        compiler_params=_cparams(("parallel",)),
        name="outproj_ln_router",
    )(*acts, x, w, g.reshape(1, -1), b.reshape(1, -1), r_hi, r_lo)


def _topk_kernel(cap, aff_ref, gate_ref, mask_ref):
    n = aff_ref.shape[1]
    bits = pltpu.bitcast(aff_ref[...], jnp.int32)
    count = lambda m: jnp.sum(m.astype(jnp.int32), axis=1, keepdims=True)

    def value_step(i, t):
        cand = t | (jnp.int32(1) << (30 - i))
        return jnp.where(count(bits >= cand) >= cap, cand, t)

    thr = lax.fori_loop(0, 31, value_step, jnp.zeros((N_EXPERTS, 1), jnp.int32))
    gt = bits > thr
    eq = bits == thr
    need = cap - count(gt)
    idx = lax.broadcasted_iota(jnp.int32, (N_EXPERTS, n), 1)
    nbits = max(int(n).bit_length(), 1)

    def index_step(i, j0):
        cand = j0 | (jnp.int32(1) << (nbits - 1 - i))
        return jnp.where(count(eq & (idx < cand)) < need, cand, j0)

    j0 = lax.fori_loop(0, nbits, index_step, jnp.zeros((N_EXPERTS, 1), jnp.int32))
    sel = gt | (eq & (idx <= j0))
    gate_ref[...] = jnp.where(sel, aff_ref[...], 0.0)
    mask_ref[...] = sel.astype(jnp.int32)


def _expert_choice(aff_t, cap):
    n = aff_t.shape[1]
    return pl.pallas_call(
        functools.partial(_topk_kernel, cap),
        out_shape=[
            jax.ShapeDtypeStruct((N_EXPERTS, n), F32),
            jax.ShapeDtypeStruct((N_EXPERTS, n), jnp.int32),
        ],
        compiler_params=pltpu.CompilerParams(vmem_limit_bytes=VMEM_LIMIT),
        name="expert_choice_topk",
    )(aff_t)


def _ffn_kernel(x_ref, g_ref, wg_ref, wu_ref, wd_ref, o_ref):
    x = _unpack_rows(x_ref[...])
    hg = jnp.dot(x, wg_ref[0], preferred_element_type=F32)
    hu = jnp.dot(x, wu_ref[0], preferred_element_type=F32)
    h = (hg * jax.nn.sigmoid(hg) * hu).astype(BF16)
    y = jnp.dot(h, wd_ref[0], preferred_element_type=F32)
    o_ref[...] = _pack_rows(y * g_ref[:, 0:1])


def _expert_ffn(xe, gate, wg, wu, wd, cap):
    rows = xe.shape[0]
    tm = _row_tile(cap, 512)
    nr = cap // tm
    wspec = pl.BlockSpec((1, D_MODEL, D_MODEL), lambda e, r: (e, 0, 0))
    return pl.pallas_call(
        _ffn_kernel,
        grid=(N_EXPERTS, nr),
        in_specs=[
            pl.BlockSpec((tm, PACKED), lambda e, r: (e * nr + r, 0)),
            pl.BlockSpec((tm, LANES), lambda e, r: (e * nr + r, 0)),
            wspec, wspec, wspec,
        ],
        out_specs=pl.BlockSpec((tm, PACKED), lambda e, r: (e * nr + r, 0)),
        out_shape=jax.ShapeDtypeStruct((rows, PACKED), U32),
        compiler_params=_cparams(("parallel", "arbitrary")),
        name="expert_ffn",
    )(xe, gate, wg, wu, wd)


def _compact_kernel(J, mask_ref, gate_ref, dest_ref, idx_ref, gsel_ref, dslot_ref):
    NC = mask_ref.shape[0]
    cap = idx_ref.shape[1]
    nt = lambda a, b: lax.dot_general(a, b, (((1,), (1,)), ((), ())), preferred_element_type=F32)
    iota = lambda shape, d: lax.broadcasted_iota(jnp.int32, shape, d)
    m = mask_ref[...].astype(F32).astype(BF16)
    tril = (iota((LANES, LANES), 1) <= iota((LANES, LANES), 0)).astype(BF16)
    eye = (iota((LANES, LANES), 1) == iota((LANES, LANES), 0)).astype(BF16)
    lt = nt(tril, m).astype(BF16)
    tot_b = jnp.dot(m, jnp.ones((LANES, LANES), BF16), preferred_element_type=F32)
    trilc = (iota((NC, NC), 1) <= iota((NC, NC), 0)).astype(BF16)
    cum_col = jnp.dot(trilc, tot_b.astype(BF16), preferred_element_type=F32)[:, 0:1]
    tot_col = tot_b[:, 0:1]
    g = gate_ref[...]
    g0 = g.astype(BF16)
    r1 = g - g0.astype(F32)
    g1 = r1.astype(BF16)
    g2 = (r1 - g1.astype(F32)).astype(BF16)
    gts = [nt(eye, p).astype(BF16) for p in (g0, g1, g2)]
    d = dest_ref[...]
    dts = [nt(eye, p.astype(F32).astype(BF16)).astype(BF16)
           for p in (d & 255, (d >> 8) & 255, d >> 16)]
    chunk_id = iota((NC, J), 0).astype(F32)
    tok_id = iota((LANES, J), 0).astype(F32)
    colsum = lambda a: jnp.sum(a, axis=0, keepdims=True)

    def block(jb, carry):
        j0 = pl.multiple_of(jb * J, J)
        slot = (j0 + iota((1, J), 1)).astype(F32)
        before = cum_col <= slot
        c_j = colsum(before.astype(F32))
        rank = slot - colsum(jnp.where(before, tot_col, 0.0))
        onehot = (chunk_id == c_j).astype(BF16)
        within = colsum((jnp.dot(lt, onehot, preferred_element_type=F32) <= rank).astype(F32))
        tsel = tok_id == within
        pick = lambda a: colsum(jnp.where(tsel, jnp.dot(a, onehot, preferred_element_type=F32), 0.0))
        idx_ref[:, pl.ds(j0, J)] = (c_j * LANES + within).astype(jnp.int32)
        gsel_ref[:, pl.ds(j0, J)] = (pick(gts[0]) + pick(gts[1])) + pick(gts[2])
        dslot_ref[:, pl.ds(j0, J)] = (pick(dts[0]) + 256.0 * pick(dts[1]) + 65536.0 * pick(dts[2])).astype(jnp.int32)
        return carry

    lax.fori_loop(0, cap // J, block, 0)


def _compact(mask_t, gate_t, dest_t, cap):
    n = mask_t.shape[1]
    nc = n // LANES
    J = _row_tile(cap, COMPACT_BLOCK)
    chunked = lambda a: a.reshape(N_EXPERTS, nc, LANES)
    ispec = pl.BlockSpec((None, nc, LANES), lambda e: (e, 0, 0))
    ospec = pl.BlockSpec((None, 1, cap), lambda e: (e, 0, 0))
    oshape = lambda dt: jax.ShapeDtypeStruct((N_EXPERTS, 1, cap), dt)
    idx, gsel, dslot = pl.pallas_call(
        functools.partial(_compact_kernel, J),
        grid=(N_EXPERTS,),
        in_specs=[ispec, ispec, ispec],
        out_specs=[ospec, ospec, ospec],
        out_shape=[oshape(jnp.int32), oshape(F32), oshape(jnp.int32)],
        compiler_params=_cparams(("parallel",)),
        name="slot_lists",
    )(chunked(mask_t), chunked(gate_t), chunked(dest_t))
    return idx.reshape(-1), gsel.reshape(-1), dslot.reshape(-1)


def _sc_rows(data, idx, gather):
    R = idx.shape[0]
    D = data.shape[1]
    info = plsc.get_sparse_core_info()
    nw = info.num_cores * info.num_subcores
    per_w = R // nw
    assert per_w * nw == R and per_w % SC_WINDOW == 0
    mesh = plsc.VectorSubcoreMesh(core_axis_name="c", subcore_axis_name="s")

    @functools.partial(
        pl.kernel, mesh=mesh,
        out_type=jax.ShapeDtypeStruct((R, D), data.dtype),
        scratch_types=[pltpu.VMEM((SC_WINDOW,), jnp.int32),
                       pltpu.VMEM((SC_WINDOW, D), data.dtype),
                       pltpu.SemaphoreType.DMA],
        name="sc_gather_rows" if gather else "sc_scatter_rows",
    )
    def move(data_hbm, idx_hbm, out_hbm, idx_v, rows_v, sem):
        wid = lax.axis_index("s") * info.num_cores + lax.axis_index("c")

        @pl.loop(0, per_w // SC_WINDOW)
        def _(j):
            base = wid * per_w + j * SC_WINDOW
            pltpu.sync_copy(idx_hbm.at[pl.ds(base, SC_WINDOW)], idx_v)
            if gather:
                pltpu.async_copy(data_hbm.at[idx_v], rows_v, sem).wait()
                pltpu.sync_copy(rows_v, out_hbm.at[pl.ds(base, SC_WINDOW)])
            else:
                pltpu.sync_copy(data_hbm.at[pl.ds(base, SC_WINDOW)], rows_v)
                pltpu.async_copy(rows_v, out_hbm.at[idx_v], sem).wait()

    return move(data, idx)


def _gather_rows(table, idx):
    return _sc_rows(table, idx, True)


def _scatter_rows(src, idx):
    return _sc_rows(src, idx, False)


def _combine_kernel(n_rows, base_ref, x_ref, off_ref, end_ref, g_ref, b_ref, z_hbm, o_ref, zbuf, sem, acc_ref):
    i = pl.program_id(0)
    start = base_ref[i]
    stop = base_ref[i + 1]
    base = (start // ROW_ALIGN) * ROW_ALIGN
    nck = (stop - base + Z_CHUNK - 1) // Z_CHUNK

    def row0(k):
        return pl.multiple_of(jnp.minimum(base + k * Z_CHUNK, n_rows - Z_CHUNK), ROW_ALIGN)

    def copy(k, slot):
        return pltpu.make_async_copy(z_hbm.at[pl.ds(row0(k), Z_CHUNK)], zbuf.at[slot], sem.at[slot])

    acc_ref[...] = jnp.zeros_like(acc_ref)

    @pl.when(nck > 0)
    def _():
        copy(0, 0).start()

    off_col = off_ref[:, 0:1]
    end_col = end_ref[:, 0:1]
    lane = lax.broadcasted_iota(jnp.int32, (1, Z_CHUNK), 1)

    def body(k, carry):
        slot = k % 2
        copy(k, slot).wait()

        @pl.when(k + 1 < nck)
        def _():
            copy(k + 1, 1 - slot).start()

        d = row0(k) + lane
        own = (d >= off_col) & (d < end_col) & (d >= base + k * Z_CHUNK)
        z = _unpack_rows(zbuf[slot])
        acc_ref[...] += jnp.dot(own.astype(BF16), z, preferred_element_type=F32)
        return carry

    lax.fori_loop(0, nck, body, 0)
    y = ALPHA * x_ref[...] + acc_ref[...]
    mu = jnp.mean(y, axis=-1, keepdims=True)
    yc = y - mu
    var = jnp.mean(yc * yc, axis=-1, keepdims=True)
    o_ref[...] = yc * lax.rsqrt(var + LN_EPS) * g_ref[...] + b_ref[...]


def _combine_ln(x1, z, off, cnt, g, b):
    n = x1.shape[0]
    n_rows = z.shape[0]
    T = _row_tile(n, TOKEN_TILE)
    assert n_rows >= Z_CHUNK and n_rows % ROW_ALIGN == 0
    tile_base = jnp.concatenate([off[::T], jnp.full((1,), n_rows, jnp.int32)])
    offb = jnp.broadcast_to(off[:, None], (n, LANES))
    endb = jnp.broadcast_to((off + cnt)[:, None], (n, LANES))
    row = lambda w: pl.BlockSpec((T, w), lambda i, tb: (i, 0))
    vec = pl.BlockSpec((1, D_MODEL), lambda i, tb: (0, 0))
    return pl.pallas_call(
        functools.partial(_combine_kernel, n_rows),
        grid_spec=pltpu.PrefetchScalarGridSpec(
            num_scalar_prefetch=1,
            grid=(n // T,),
            in_specs=[row(D_MODEL), row(LANES), row(LANES), vec, vec, pl.BlockSpec(memory_space=pl.ANY)],
            out_specs=row(D_MODEL),
            scratch_shapes=[pltpu.VMEM((2, Z_CHUNK, PACKED), U32), pltpu.SemaphoreType.DMA((2,)),
                            pltpu.VMEM((T, D_MODEL), F32)],
        ),
        out_shape=jax.ShapeDtypeStruct((n, D_MODEL), F32),
        compiler_params=_cparams(("arbitrary",)),
        name="combine_ln",
    )(tile_base, x1, offb, endb, g.reshape(1, -1), b.reshape(1, -1), z)


def _inproj1_kernel(x_ref, w_ref, tab_ref, gq_ref, gk_ref, seg_ref, q_ref, k_ref, v_ref):
    h = jnp.dot(x_ref[...].astype(BF16), w_ref[...], preferred_element_type=F32)
    sh = HEAD_DIM // 4
    seg = seg_ref[...]

    def norm_rope(xc, gain, t0):
        ms = jnp.dot((xc * xc).astype(BF16), seg, preferred_element_type=F32)
        xn = xc * lax.rsqrt(ms + RMS_EPS) * gain
        return _rope_chunk(xn, tab_ref[t0], tab_ref[t0 + 1], tab_ref[t0 + 2], sh)

    for c in range(C_Q // LANES):
        xc = h[:, c * LANES:(c + 1) * LANES]
        q_ref[:, c * LANES:(c + 1) * LANES] = norm_rope(xc, gq_ref[...], 0).astype(BF16)
    for c in range(C_KV // LANES):
        xc = h[:, C_Q + c * LANES:C_Q + (c + 1) * LANES]
        kc = norm_rope(xc, gk_ref[...], 3).astype(BF16)
        k_ref[2 * c] = kc[:, :HEAD_DIM]
        k_ref[2 * c + 1] = kc[:, HEAD_DIM:]
        vc = h[:, C_Q + C_KV + c * LANES:C_Q + C_KV + (c + 1) * LANES].astype(BF16)
        v_ref[2 * c] = vc[:, :HEAD_DIM]
        v_ref[2 * c + 1] = vc[:, HEAD_DIM:]


def _inproj1(x, w, tab, gq, gk, S):
    n = x.shape[0]
    tm = _row_tile(S, 512)
    nt = S // tm
    lane_head = np.arange(LANES) // HEAD_DIM
    seg = jnp.asarray((lane_head[:, None] == lane_head[None, :]) / HEAD_DIM, BF16)
    tile2 = lambda g: jnp.concatenate([g, g]).reshape(1, LANES)
    const = lambda shape: pl.BlockSpec(shape, lambda i: (0,) * len(shape))
    return pl.pallas_call(
        _inproj1_kernel,
        grid=(n // tm,),
        in_specs=[
            pl.BlockSpec((tm, D_MODEL), lambda i: (i, 0)),
            const((D_MODEL, C_IN)),
            pl.BlockSpec((6, tm, LANES), lambda i: (0, i % nt, 0)),
            const((1, LANES)), const((1, LANES)), const((LANES, LANES)),
        ],
        out_specs=[
            pl.BlockSpec((tm, C_Q), lambda i: (i, 0)),
            pl.BlockSpec((AX_KV_HEADS, tm, HEAD_DIM), lambda i: (0, i, 0)),
            pl.BlockSpec((AX_KV_HEADS, tm, HEAD_DIM), lambda i: (0, i, 0)),
        ],
        out_shape=[
            jax.ShapeDtypeStruct((n, C_Q), BF16),
            jax.ShapeDtypeStruct((AX_KV_HEADS, n, HEAD_DIM), BF16),
            jax.ShapeDtypeStruct((AX_KV_HEADS, n, HEAD_DIM), BF16),
        ],
        compiler_params=_cparams(("parallel",)),
        name="inproj1",
    )(x, w, tab, tile2(gq), tile2(gk), seg)


def _axattn_kernel(q_ref, k_ref, v_ref, o_ref):
    G = AX_HEADS // AX_KV_HEADS
    TQ = q_ref.shape[0]
    for r in range(TQ // BLOCK):
        rows = slice(r * BLOCK, (r + 1) * BLOCK)
        for j in range(G):
            cols = slice(j * HEAD_DIM, (j + 1) * HEAD_DIM)
            s = lax.dot_general(q_ref[rows, cols], k_ref[...], (((1,), (1,)), ((), ())),
                                preferred_element_type=F32)
            m = jnp.max(s, axis=-1, keepdims=True)
            p = jnp.exp(s - m)
            den = jnp.sum(p, axis=-1, keepdims=True)
            o = jnp.dot(p.astype(BF16), v_ref[...], preferred_element_type=F32) / den
            o_ref[rows, cols] = o.astype(BF16)


def _axial_attention(q, k, v, B, S):
    n = q.shape[0]
    TQ = _row_tile(S, AX_Q_TILE)
    nq = S // TQ
    G = AX_HEADS // AX_KV_HEADS
    qspec = pl.BlockSpec((TQ, G * HEAD_DIM), lambda b, g, i: (b * nq + i, g))
    kvspec = pl.BlockSpec((None, None, S, HEAD_DIM), lambda b, g, i: (g, b, 0, 0))
    return pl.pallas_call(
        _axattn_kernel,
        grid=(B, AX_KV_HEADS, nq),
        in_specs=[qspec, kvspec, kvspec],
        out_specs=qspec,
        out_shape=jax.ShapeDtypeStruct((n, C_Q), BF16),
        compiler_params=_cparams(("parallel", "parallel", "arbitrary")),
        name="axial_attention",
    )(q, k, v)


def _moe(x1, xb, aff, wg, wu, wd, ln_g, ln_b, n):
    cap = CAPACITY_FACTOR * n // N_EXPERTS
    gate_t, mask_t = _expert_choice(aff.T, cap)
    cnt = jnp.sum(mask_t, axis=0)
    off = jnp.cumsum(cnt) - cnt
    dest_t = off[None, :] + jnp.cumsum(mask_t, axis=0) - mask_t
    idx, gsel, dslot = _compact(mask_t, gate_t, dest_t, cap)
    xe = _gather_rows(xb, idx)
    gb = jnp.broadcast_to(gsel[:, None], (gsel.shape[0], LANES))
    ye = _expert_ffn(xe, gb, wg, wu, wd, cap)
    z = _scatter_rows(ye, dslot)
    return _combine_ln(x1, z, off, cnt, ln_g, ln_b)


def _trunk(x, p):
    B, S, _ = x.shape
    n = B * S
    xf = x.reshape(n, D_MODEL)
    tab_w = _tables_window(S)
    tab_a = _tables_axial(S)
    for layer in range(DEPTH):
        i = layer // 2
        if layer % 2 == 0:
            ug, q, k, v = _inproj0(xf, p['ab_w_in'][i], tab_w, S)
            conv = _conv_branch(ug, p['ab_conv_w'][i], p['ab_conv_b'][i], p['ab_conv_ln_g'][i],
                                p['ab_conv_ln_b'][i], S)
            att = _window_attention(q, k, v, p['ab_sink'][i], B, S)
            acts, w_out = [conv, att], p['ab_w_out'][i]
        else:
            q, k, v = _inproj1(xf, p['c_w_in'][i], tab_a, p['c_q_norm'][i], p['c_k_norm'][i], S)
            k = k.reshape(AX_KV_HEADS, B, S, HEAD_DIM)
            v = v.reshape(AX_KV_HEADS, B, S, HEAD_DIM)
            acts, w_out = [_axial_attention(q, k, v, B, S)], p['c_w_out'][i]
        x1, xb, aff = _outproj_ln_router(acts, xf, w_out, p['ln1_g'][layer], p['ln1_b'][layer],
                                         p['moe_router'][layer])
        xf = _moe(x1, xb, aff, p['moe_w_gate'][layer], p['moe_w_up'][layer], p['moe_w_down'][layer],
                  p['ln2_g'][layer], p['ln2_b'][layer], n)
    return xf.reshape(B, S, D_MODEL)


def kernel(x_prompt, x_sample, ab_w_in, ab_conv_w, ab_conv_b, ab_conv_ln_g, ab_conv_ln_b, ab_sink, ab_w_out, c_w_in, c_q_norm, c_k_norm, c_w_out, ln1_g, ln1_b, moe_router, moe_w_gate, moe_w_up, moe_w_down, ln2_g, ln2_b):
    p = dict(
        ab_w_in=ab_w_in.astype(BF16), ab_conv_w=ab_conv_w, ab_conv_b=ab_conv_b,
        ab_conv_ln_g=ab_conv_ln_g, ab_conv_ln_b=ab_conv_ln_b, ab_sink=ab_sink,
        ab_w_out=ab_w_out.astype(BF16), c_w_in=c_w_in.astype(BF16), c_q_norm=c_q_norm,
        c_k_norm=c_k_norm, c_w_out=c_w_out.astype(BF16), ln1_g=ln1_g, ln1_b=ln1_b,
        moe_router=moe_router, moe_w_gate=moe_w_gate.astype(BF16), moe_w_up=moe_w_up.astype(BF16),
        moe_w_down=moe_w_down.astype(BF16), ln2_g=ln2_g, ln2_b=ln2_b,
    )
    return (_trunk(x_prompt, p), _trunk(x_sample, p))
```

```python
import functools

import jax
import jax.numpy as jnp
import numpy as np
from jax import lax
from jax.experimental import pallas as pl
from jax.experimental.pallas import tpu as pltpu
from jax.experimental.pallas import tpu_sc as plsc

D_MODEL = 1024
DEPTH = 2
GRID_W = 64
BLOCK = 128
HEAD_DIM = 64
CONV_DIM = 512
CONV_WIDTH = 31
WIN_HEADS = 8
WIN_KV_HEADS = 2
WINDOW = 128
ROT_DIM_B = 16
ROPE_THETA_B = 500000.0
AX_HEADS = 16
AX_KV_HEADS = 4
ROPE_THETA_C = 10000.0
N_EXPERTS = 16
CAPACITY_FACTOR = 2
ALPHA = (2 * DEPTH) ** 0.25
AB_Q = WIN_HEADS * HEAD_DIM
AB_KV = WIN_KV_HEADS * HEAD_DIM
AB_IN = 2 * CONV_DIM + AB_Q + 2 * AB_KV
C_Q = AX_HEADS * HEAD_DIM
C_KV = AX_KV_HEADS * HEAD_DIM
C_IN = C_Q + 2 * C_KV
LN_EPS = 1e-5
LOG2E = 1.4426950408889634
RMS_EPS = 1e-6

LANES = 128
SUBLANES = 8
CONV_ROWS = 64
CONV_SPAN = 24
MM_ROWS = 128
CONV_HALO = 16
VMEM_LIMIT = 48 * 1024 * 1024

TOKEN_TILE = 256
Z_CHUNK = 256
Z_SLOTS = 4
ROW_ALIGN = 16
SC_WINDOW = 64
COMPACT_BLOCK = 256
AX_Q_TILE = 512
AX_CHAIN_ROWS = 256

BF16 = jnp.bfloat16
F32 = jnp.float32
U32 = jnp.uint32
PACKED = D_MODEL // 2


def _pack_rows(x):
    b = lax.bitcast_convert_type(x.astype(BF16).astype(F32), U32)
    return (b[:, :PACKED] >> 16) | (b[:, PACKED:] & jnp.uint32(0xFFFF0000))


def _unpack_rows(w):
    lo = lax.bitcast_convert_type(w << 16, F32)
    hi = lax.bitcast_convert_type(w & jnp.uint32(0xFFFF0000), F32)
    return jnp.concatenate([lo, hi], axis=1).astype(BF16)


def _cparams(sem):
    return pltpu.CompilerParams(dimension_semantics=sem, vmem_limit_bytes=VMEM_LIMIT)


def _row_tile(n, pref):
    t = min(n, pref)
    assert n % t == 0
    return t


def _rope_tables(cos, sin, lo, shift, scale):
    S = cos.shape[0]
    c = jnp.ones((S, HEAD_DIM), F32)
    sa = jnp.zeros((S, HEAD_DIM), F32)
    sb = jnp.zeros((S, HEAD_DIM), F32)
    c = c.at[:, lo:lo + shift].set(cos).at[:, lo + shift:lo + 2 * shift].set(cos)
    sa = sa.at[:, lo:lo + shift].set(-sin)
    sb = sb.at[:, lo + shift:lo + 2 * shift].set(sin)
    return c * scale, sa * scale, sb * scale


def _angles(pos, dim, theta):
    inv = theta ** (-jnp.arange(0, dim, 2, dtype=F32) / dim)
    return pos[:, None] * inv[None, :]


def _tables_window(S):
    ang = _angles(jnp.arange(S, dtype=F32), ROT_DIM_B, ROPE_THETA_B)
    cos, sin = jnp.cos(ang), jnp.sin(ang)
    out = []
    for scale in (HEAD_DIM ** -0.5, 1.0):
        c, sa, sb = _rope_tables(cos, sin, 0, ROT_DIM_B // 2, scale)
        out += [c, sa, sb]
    t = jnp.stack(out)
    return jnp.concatenate([t, t], axis=-1)


def _tables_axial(S):
    rows = S // GRID_W
    gr, gc = jnp.meshgrid(jnp.arange(rows), jnp.arange(GRID_W), indexing='ij')
    ang_r = _angles(gr.reshape(-1).astype(F32), HEAD_DIM // 2, ROPE_THETA_C)
    ang_c = _angles(gc.reshape(-1).astype(F32), HEAD_DIM // 2, ROPE_THETA_C)
    out = []
    for scale in (HEAD_DIM ** -0.5 * LOG2E, 1.0):
        c1, sa1, sb1 = _rope_tables(jnp.cos(ang_r), jnp.sin(ang_r), 0, HEAD_DIM // 4, scale)
        c2, sa2, sb2 = _rope_tables(jnp.cos(ang_c), jnp.sin(ang_c), HEAD_DIM // 2, HEAD_DIM // 4, scale)
        half = jnp.arange(HEAD_DIM) < HEAD_DIM // 2
        out += [jnp.where(half, c1, c2), sa1 + sa2, sb1 + sb2]
    t = jnp.stack(out)
    return jnp.concatenate([t, t], axis=-1)


def _rope_chunk(x, c, sa, sb, shift):
    fwd = pltpu.roll(x, LANES - shift, axis=1)
    bwd = pltpu.roll(x, shift, axis=1)
    return x * c + fwd * sa + bwd * sb


def _inproj0_kernel(x_ref, w_ref, tab_ref, ug_ref, q_ref, k_ref, v_ref):
    h = jnp.dot(x_ref[...].astype(BF16), w_ref[...], preferred_element_type=F32)
    ug_ref[...] = h[:, :2 * CONV_DIM].astype(BF16)
    sh = ROT_DIM_B // 2
    q0 = 2 * CONV_DIM
    for c in range(AB_Q // LANES):
        xc = h[:, q0 + c * LANES:q0 + (c + 1) * LANES]
        q_ref[:, c * LANES:(c + 1) * LANES] = _rope_chunk(
            xc, tab_ref[0], tab_ref[1], tab_ref[2], sh).astype(BF16)
    k0 = q0 + AB_Q
    for c in range(AB_KV // LANES):
        xc = h[:, k0 + c * LANES:k0 + (c + 1) * LANES]
        k_ref[:, c * LANES:(c + 1) * LANES] = _rope_chunk(
            xc, tab_ref[3], tab_ref[4], tab_ref[5], sh).astype(BF16)
    v_ref[...] = h[:, k0 + AB_KV:].astype(BF16)


def _inproj0(x, w, tab, S):
    n = x.shape[0]
    tm = _row_tile(S, 512)
    nt = S // tm
    return pl.pallas_call(
        _inproj0_kernel,
        grid=(n // tm,),
        in_specs=[
            pl.BlockSpec((tm, D_MODEL), lambda i: (i, 0)),
            pl.BlockSpec((D_MODEL, AB_IN), lambda i: (0, 0)),
            pl.BlockSpec((6, tm, LANES), lambda i: (0, i % nt, 0)),
        ],
        out_specs=[
            pl.BlockSpec((tm, 2 * CONV_DIM), lambda i: (i, 0)),
            pl.BlockSpec((tm, AB_Q), lambda i: (i, 0)),
            pl.BlockSpec((tm, AB_KV), lambda i: (i, 0)),
            pl.BlockSpec((tm, AB_KV), lambda i: (i, 0)),
        ],
        out_shape=[
            jax.ShapeDtypeStruct((n, 2 * CONV_DIM), BF16),
            jax.ShapeDtypeStruct((n, AB_Q), BF16),
            jax.ShapeDtypeStruct((n, AB_KV), BF16),
            jax.ShapeDtypeStruct((n, AB_KV), BF16),
        ],
        compiler_params=_cparams(("parallel",)),
        name="inproj0",
    )(x, w, tab)


def _conv_kernel(nt, cur_ref, prev_ref, next_ref, w_ref, b_ref, g_ref, beta_ref, o_ref, buf_ref, shift_ref, hc_ref):
    T = cur_ref.shape[0]
    si = pl.program_id(0) % nt

    def glu(blk):
        u = blk[:, :CONV_DIM].astype(F32)
        gt = blk[:, CONV_DIM:].astype(F32)
        return u * jax.nn.sigmoid(gt)

    pv = jnp.where(si > 0, 1.0, 0.0).astype(F32)
    nv = jnp.where(si < nt - 1, 1.0, 0.0).astype(F32)
    buf_ref[0:CONV_HALO, :] = glu(prev_ref[...]) * pv
    buf_ref[CONV_HALO:CONV_HALO + T, :] = glu(cur_ref[...])
    buf_ref[CONV_HALO + T:, :] = glu(next_ref[...]) * nv
    L = shift_ref.shape[1]
    for r in range(1, SUBLANES):
        shift_ref[r - 1] = buf_ref[r:r + L, :]

    off = CONV_HALO - CONV_WIDTH // 2

    def rows(rb, carry):
        r0 = pl.multiple_of(rb * CONV_ROWS, CONV_ROWS)
        for c in range(CONV_DIM // LANES):
            cols = slice(c * LANES, (c + 1) * LANES)
            acc = jnp.zeros((CONV_ROWS, LANES), F32)
            for r in range(SUBLANES):
                src = buf_ref if r == 0 else shift_ref.at[r - 1]
                xr = src[pl.ds(r0, CONV_ROWS + CONV_SPAN), cols]
                for a in range(CONV_SPAN // SUBLANES + 1):
                    j = SUBLANES * a + r - off
                    if 0 <= j < CONV_WIDTH:
                        wj = w_ref[j * SUBLANES:(j + 1) * SUBLANES, cols]
                        wt = jnp.concatenate([wj] * (CONV_ROWS // SUBLANES), axis=0)
                        acc = acc + xr[SUBLANES * a:SUBLANES * a + CONV_ROWS] * wt
            hc_ref[pl.ds(r0, CONV_ROWS), cols] = acc + b_ref[:, cols]
        return carry

    lax.fori_loop(0, T // CONV_ROWS, rows, 0)
    hcv = hc_ref[...]
    mu = jnp.mean(hcv, axis=-1, keepdims=True)
    xc = hcv - mu
    var = jnp.mean(xc * xc, axis=-1, keepdims=True)
    y = xc * lax.rsqrt(var + LN_EPS) * g_ref[...] + beta_ref[...]
    o_ref[...] = (y * jax.nn.sigmoid(y)).astype(BF16)


def _conv_branch(ug, w, b, g, beta, S):
    n = ug.shape[0]
    T = _row_tile(S, 512)
    nt = S // T
    r = T // CONV_HALO
    nh = n // CONV_HALO
    vec = lambda: pl.BlockSpec((1, CONV_DIM), lambda i: (0, 0))
    return pl.pallas_call(
        functools.partial(_conv_kernel, nt),
        grid=(n // T,),
        in_specs=[
            pl.BlockSpec((T, 2 * CONV_DIM), lambda i: (i, 0)),
            pl.BlockSpec((CONV_HALO, 2 * CONV_DIM), lambda i: (jnp.maximum(i * r - 1, 0), 0)),
            pl.BlockSpec((CONV_HALO, 2 * CONV_DIM), lambda i: (jnp.minimum((i + 1) * r, nh - 1), 0)),
            pl.BlockSpec((CONV_WIDTH * SUBLANES, CONV_DIM), lambda i: (0, 0)),
            vec(), vec(), vec(),
        ],
        out_specs=pl.BlockSpec((T, CONV_DIM), lambda i: (i, 0)),
        out_shape=jax.ShapeDtypeStruct((n, CONV_DIM), BF16),
        scratch_shapes=[pltpu.VMEM((T + 2 * CONV_HALO, CONV_DIM), F32),
                        pltpu.VMEM((SUBLANES - 1, T + 2 * CONV_HALO - SUBLANES, CONV_DIM), F32),
                        pltpu.VMEM((T, CONV_DIM), F32)],
        compiler_params=_cparams(("parallel",)),
        name="conv_branch",
    )(ug, ug, ug, jnp.repeat(w, SUBLANES, axis=0), b.reshape(1, -1), g.reshape(1, -1), beta.reshape(1, -1))


def _winattn_kernel(nq, sink_ref, q_ref, kp_ref, kc_ref, kn_ref, vp_ref, vc_ref, vn_ref, o_ref):
    qi = pl.program_id(1)
    G = WIN_HEADS // WIN_KV_HEADS
    M = G * BLOCK
    row = lax.broadcasted_iota(jnp.int32, (M, 3 * BLOCK), 0) % BLOCK
    col = lax.broadcasted_iota(jnp.int32, (M, 3 * BLOCK), 1)
    band = jnp.abs(col - BLOCK - row) <= WINDOW
    head = lax.broadcasted_iota(jnp.int32, (M, 1), 0) // BLOCK
    for half in range(2):
        rows = slice(half * BLOCK, (half + 1) * BLOCK)
        if half == 0:
            valid = band & ((qi > 0) | (col >= BLOCK))
            krefs, vrefs = (kp_ref, kc_ref), (vp_ref, vc_ref)
        else:
            valid = band & ((qi < nq - 1) | (col < 2 * BLOCK))
            krefs, vrefs = (kc_ref, kn_ref), (vc_ref, vn_ref)
        for g in range(WIN_KV_HEADS):
            sl = slice(g * HEAD_DIM, (g + 1) * HEAD_DIM)
            kk = jnp.concatenate([r[:, sl] for r in krefs], axis=0)
            vv = jnp.concatenate([r[:, sl] for r in vrefs], axis=0)
            heads = range(g * G, (g + 1) * G)
            qs = jnp.concatenate([q_ref[rows, h * HEAD_DIM:(h + 1) * HEAD_DIM] for h in heads], axis=0)
            s = lax.dot_general(qs, kk, (((1,), (1,)), ((), ())), preferred_element_type=F32)
            s = jnp.where(valid, s, -jnp.inf)
            sk = jnp.zeros((M, 1), F32)
            for j, h in enumerate(heads):
                sk = jnp.where(head == j, sink_ref[h], sk)
            m = jnp.maximum(jnp.max(s, axis=-1, keepdims=True), sk)
            p = jnp.exp(s - m)
            den = jnp.sum(p, axis=-1, keepdims=True) + jnp.exp(sk - m)
            o = jnp.dot(p.astype(BF16), vv, preferred_element_type=F32) / den
            for j, h in enumerate(heads):
                o_ref[rows, h * HEAD_DIM:(h + 1) * HEAD_DIM] = o[j * BLOCK:(j + 1) * BLOCK].astype(BF16)


def _window_attention(q, k, v, sink, B, S):
    n = q.shape[0]
    nb = S // BLOCK
    nq = nb // 2
    cur = lambda b, i: (b * nq + i, 0)
    prev = lambda b, i: (b * nb + jnp.maximum(2 * i - 1, 0), 0)
    nxt = lambda b, i: (b * nb + jnp.minimum(2 * i + 2, nb - 1), 0)
    edge = lambda f: pl.BlockSpec((BLOCK, AB_KV), f)
    mid = pl.BlockSpec((2 * BLOCK, AB_KV), cur)
    return pl.pallas_call(
        functools.partial(_winattn_kernel, nq),
        grid=(B, nq),
        in_specs=[
            pl.BlockSpec(memory_space=pltpu.SMEM),
            pl.BlockSpec((2 * BLOCK, AB_Q), cur),
            edge(prev), mid, edge(nxt), edge(prev), mid, edge(nxt),
        ],
        out_specs=pl.BlockSpec((2 * BLOCK, AB_Q), cur),
        out_shape=jax.ShapeDtypeStruct((n, AB_Q), BF16),
        compiler_params=_cparams(("parallel", "parallel")),
        name="window_attention",
    )(sink, q, k, k, k, v, v, v)


def _outproj_kernel(n_act, *refs):
    acts = refs[:n_act]
    x_ref, w_ref, g_ref, b_ref, rcat_ref, rhi_ref, xo_ref, xb_ref, aff_ref = refs[n_act:]
    tm = x_ref.shape[0]
    sub = min(tm, MM_ROWS)
    for r in range(tm // sub):
        rows = slice(r * sub, (r + 1) * sub)
        a = acts[0][rows, :] if n_act == 1 else jnp.concatenate([ref[rows, :] for ref in acts], axis=-1)
        mix = jnp.dot(a, w_ref[...], preferred_element_type=F32)
        y = ALPHA * x_ref[rows, :] + mix
        mu = jnp.mean(y, axis=-1, keepdims=True)
        yc = y - mu
        var = jnp.mean(yc * yc, axis=-1, keepdims=True)
        x1 = yc * lax.rsqrt(var + LN_EPS) * g_ref[...] + b_ref[...]
        xo_ref[rows, :] = x1
        xb_ref[rows, :] = _pack_rows(x1)
        x_hi = x1.astype(BF16)
        x_lo = (x1 - x_hi.astype(F32)).astype(BF16)
        t = jnp.dot(x_hi, rcat_ref[...], preferred_element_type=F32)
        logits = (t[:, :N_EXPERTS] + jnp.dot(x_lo, rhi_ref[...], preferred_element_type=F32)) + t[:, N_EXPERTS:]
        m = jnp.max(logits, axis=-1, keepdims=True)
        e = jnp.exp(logits - m)
        aff_ref[rows, :] = e / jnp.sum(e, axis=-1, keepdims=True)


def _outproj_ln_router(acts, x, w, g, b, w_r):
    n = x.shape[0]
    tm = _row_tile(n, 512)
    r_hi = w_r.astype(BF16)
    r_lo = (w_r - r_hi.astype(F32)).astype(BF16)
    row = lambda width: pl.BlockSpec((tm, width), lambda i: (i, 0))
    const = lambda shape: pl.BlockSpec(shape, lambda i: (0, 0))
    return pl.pallas_call(
        functools.partial(_outproj_kernel, len(acts)),
        grid=(n // tm,),
        in_specs=[row(a.shape[1]) for a in acts] + [
            row(D_MODEL), const((D_MODEL, D_MODEL)), const((1, D_MODEL)), const((1, D_MODEL)),
            const((D_MODEL, 2 * N_EXPERTS)), const((D_MODEL, N_EXPERTS)),
        ],
        out_specs=[row(D_MODEL), row(PACKED), row(N_EXPERTS)],
        out_shape=[
            jax.ShapeDtypeStruct((n, D_MODEL), F32),
            jax.ShapeDtypeStruct((n, PACKED), U32),
            jax.ShapeDtypeStruct((n, N_EXPERTS), F32),
        ],
        compiler_params=_cparams(("parallel",)),
        name="outproj_ln_router",
    )(*acts, x, w, g.reshape(1, -1), b.reshape(1, -1), jnp.concatenate([r_hi, r_lo], axis=1), r_hi)


def _topk_kernel(cap, aff_ref, gate_ref, mask_ref):
    n = aff_ref.shape[1]
    bits = pltpu.bitcast(aff_ref[...], jnp.int32)
    count = lambda m: jnp.sum(m.astype(jnp.int32), axis=1, keepdims=True)

    def value_step(i, t):
        cand = t | (jnp.int32(1) << (30 - i))
        return jnp.where(count(bits >= cand) >= cap, cand, t)

    thr = lax.fori_loop(0, 31, value_step, jnp.zeros((N_EXPERTS, 1), jnp.int32))
    gt = bits > thr
    eq = bits == thr
    need = cap - count(gt)
    idx = lax.broadcasted_iota(jnp.int32, (N_EXPERTS, n), 1)
    nbits = max(int(n).bit_length(), 1)

    def index_step(i, j0):
        cand = j0 | (jnp.int32(1) << (nbits - 1 - i))
        return jnp.where(count(eq & (idx < cand)) < need, cand, j0)

    j0 = lax.fori_loop(0, nbits, index_step, jnp.zeros((N_EXPERTS, 1), jnp.int32))
    sel = gt | (eq & (idx <= j0))
    gate_ref[...] = jnp.where(sel, aff_ref[...], 0.0)
    mask_ref[...] = sel.astype(jnp.int32)


def _expert_choice(aff_t, cap):
    n = aff_t.shape[1]
    return pl.pallas_call(
        functools.partial(_topk_kernel, cap),
        out_shape=[
            jax.ShapeDtypeStruct((N_EXPERTS, n), F32),
            jax.ShapeDtypeStruct((N_EXPERTS, n), jnp.int32),
        ],
        compiler_params=pltpu.CompilerParams(vmem_limit_bytes=VMEM_LIMIT),
        name="expert_choice_topk",
    )(aff_t)


def _ffn_kernel(x_ref, g_ref, wg_ref, wu_ref, wd_ref, o_ref):
    x = _unpack_rows(x_ref[...])
    hg = jnp.dot(x, wg_ref[0], preferred_element_type=F32)
    hu = jnp.dot(x, wu_ref[0], preferred_element_type=F32)
    h = (hg * jax.nn.sigmoid(hg) * hu).astype(BF16)
    y = jnp.dot(h, wd_ref[0], preferred_element_type=F32)
    o_ref[...] = _pack_rows(y * g_ref[:, 0:1])


def _expert_ffn(xe, gate, wg, wu, wd, cap):
    rows = xe.shape[0]
    tm = _row_tile(cap, 512)
    nr = cap // tm
    wspec = pl.BlockSpec((1, D_MODEL, D_MODEL), lambda e, r: (e, 0, 0))
    return pl.pallas_call(
        _ffn_kernel,
        grid=(N_EXPERTS, nr),
        in_specs=[
            pl.BlockSpec((tm, PACKED), lambda e, r: (e * nr + r, 0)),
            pl.BlockSpec((tm, LANES), lambda e, r: (e * nr + r, 0)),
            wspec, wspec, wspec,
        ],
        out_specs=pl.BlockSpec((tm, PACKED), lambda e, r: (e * nr + r, 0)),
        out_shape=jax.ShapeDtypeStruct((rows, PACKED), U32),
        compiler_params=_cparams(("parallel", "arbitrary")),
        name="expert_ffn",
    )(xe, gate, wg, wu, wd)


def _compact_kernel(J, mask_ref, gate_ref, dest_ref, idx_ref, gsel_ref, dslot_ref):
    NC = mask_ref.shape[0]
    cap = idx_ref.shape[1]
    nt = lambda a, b: lax.dot_general(a, b, (((1,), (1,)), ((), ())), preferred_element_type=F32)
    iota = lambda shape, d: lax.broadcasted_iota(jnp.int32, shape, d)
    m = mask_ref[...].astype(F32).astype(BF16)
    tril = (iota((LANES, LANES), 1) <= iota((LANES, LANES), 0)).astype(BF16)
    eye = (iota((LANES, LANES), 1) == iota((LANES, LANES), 0)).astype(BF16)
    lt = nt(tril, m).astype(BF16)
    tot_b = jnp.dot(m, jnp.ones((LANES, LANES), BF16), preferred_element_type=F32)
    trilc = (iota((NC, NC), 1) <= iota((NC, NC), 0)).astype(BF16)
    cum_col = jnp.dot(trilc, tot_b.astype(BF16), preferred_element_type=F32)[:, 0:1]
    tot_col = tot_b[:, 0:1]
    g = gate_ref[...]
    g0 = g.astype(BF16)
    r1 = g - g0.astype(F32)
    g1 = r1.astype(BF16)
    g2 = (r1 - g1.astype(F32)).astype(BF16)
    gts = [nt(eye, p).astype(BF16) for p in (g0, g1, g2)]
    d = dest_ref[...]
    dts = [nt(eye, p.astype(F32).astype(BF16)).astype(BF16)
           for p in (d & 255, (d >> 8) & 255, d >> 16)]
    chunk_id = iota((NC, J), 0).astype(F32)
    tok_id = iota((LANES, J), 0).astype(F32)
    colsum = lambda a: jnp.sum(a, axis=0, keepdims=True)

    def block(jb, carry):
        j0 = pl.multiple_of(jb * J, J)
        slot = (j0 + iota((1, J), 1)).astype(F32)
        before = cum_col <= slot
        c_j = colsum(before.astype(F32))
        rank = slot - colsum(jnp.where(before, tot_col, 0.0))
        onehot = (chunk_id == c_j).astype(BF16)
        within = colsum((jnp.dot(lt, onehot, preferred_element_type=F32) <= rank).astype(F32))
        tsel = tok_id == within
        pick = lambda a: colsum(jnp.where(tsel, jnp.dot(a, onehot, preferred_element_type=F32), 0.0))
        idx_ref[:, pl.ds(j0, J)] = (c_j * LANES + within).astype(jnp.int32)
        gsel_ref[:, pl.ds(j0, J)] = (pick(gts[0]) + pick(gts[1])) + pick(gts[2])
        dslot_ref[:, pl.ds(j0, J)] = (pick(dts[0]) + 256.0 * pick(dts[1]) + 65536.0 * pick(dts[2])).astype(jnp.int32)
        return carry

    lax.fori_loop(0, cap // J, block, 0)


def _compact(mask_t, gate_t, dest_t, cap):
    n = mask_t.shape[1]
    nc = n // LANES
    J = _row_tile(cap, COMPACT_BLOCK)
    chunked = lambda a: a.reshape(N_EXPERTS, nc, LANES)
    ispec = pl.BlockSpec((None, nc, LANES), lambda e: (e, 0, 0))
    ospec = pl.BlockSpec((None, 1, cap), lambda e: (e, 0, 0))
    oshape = lambda dt: jax.ShapeDtypeStruct((N_EXPERTS, 1, cap), dt)
    idx, gsel, dslot = pl.pallas_call(
        functools.partial(_compact_kernel, J),
        grid=(N_EXPERTS,),
        in_specs=[ispec, ispec, ispec],
        out_specs=[ospec, ospec, ospec],
        out_shape=[oshape(jnp.int32), oshape(F32), oshape(jnp.int32)],
        compiler_params=_cparams(("parallel",)),
        name="slot_lists",
    )(chunked(mask_t), chunked(gate_t), chunked(dest_t))
    return idx.reshape(-1), gsel.reshape(-1), dslot.reshape(-1)


def _sc_rows(data, idx, gather):
    R = idx.shape[0]
    D = data.shape[1]
    info = plsc.get_sparse_core_info()
    nw = info.num_cores * info.num_subcores
    per_w = R // nw
    assert per_w * nw == R and per_w % SC_WINDOW == 0
    mesh = plsc.VectorSubcoreMesh(core_axis_name="c", subcore_axis_name="s")

    @functools.partial(
        pl.kernel, mesh=mesh,
        out_type=jax.ShapeDtypeStruct((R, D), data.dtype),
        scratch_types=[pltpu.VMEM((SC_WINDOW,), jnp.int32),
                       pltpu.VMEM((SC_WINDOW, D), data.dtype),
                       pltpu.SemaphoreType.DMA],
        name="sc_gather_rows" if gather else "sc_scatter_rows",
    )
    def move(data_hbm, idx_hbm, out_hbm, idx_v, rows_v, sem):
        wid = lax.axis_index("s") * info.num_cores + lax.axis_index("c")

        @pl.loop(0, per_w // SC_WINDOW)
        def _(j):
            base = wid * per_w + j * SC_WINDOW
            pltpu.sync_copy(idx_hbm.at[pl.ds(base, SC_WINDOW)], idx_v)
            if gather:
                pltpu.async_copy(data_hbm.at[idx_v], rows_v, sem).wait()
                pltpu.sync_copy(rows_v, out_hbm.at[pl.ds(base, SC_WINDOW)])
            else:
                pltpu.sync_copy(data_hbm.at[pl.ds(base, SC_WINDOW)], rows_v)
                pltpu.async_copy(rows_v, out_hbm.at[idx_v], sem).wait()

    return move(data, idx)


def _gather_rows(table, idx):
    return _sc_rows(table, idx, True)


def _scatter_rows(src, idx):
    return _sc_rows(src, idx, False)


def _combine_kernel(n_rows, n_tiles, base_ref, x_ref, off_ref, end_ref, g_ref, b_ref, z_hbm, o_ref,
                    zbuf, sem, acc_ref):
    i = pl.program_id(0)

    def span(t):
        b0 = (base_ref[t] // ROW_ALIGN) * ROW_ALIGN
        return b0, (base_ref[t + 1] - b0 + Z_CHUNK - 1) // Z_CHUNK

    def row0(b0, k):
        return pl.multiple_of(jnp.minimum(b0 + k * Z_CHUNK, n_rows - Z_CHUNK), ROW_ALIGN)

    def copy(b0, k, slot):
        return pltpu.make_async_copy(z_hbm.at[pl.ds(row0(b0, k), Z_CHUNK)], zbuf.at[slot], sem.at[slot])

    def prime(t):
        b0, cnt = span(t)
        for s in range(Z_SLOTS):
            @pl.when(s < cnt)
            def _():
                copy(b0, s, s).start()

    @pl.when(i == 0)
    def _():
        prime(0)

    base, nck = span(i)
    acc_ref[...] = jnp.zeros_like(acc_ref)
    off_col = off_ref[:, 0:1]
    end_col = end_ref[:, 0:1]
    lane = lax.broadcasted_iota(jnp.int32, (1, Z_CHUNK), 1)

    def body(k, carry):
        slot = k % Z_SLOTS
        copy(base, k, slot).wait()

        @pl.when((k >= 1) & (k - 1 + Z_SLOTS < nck))
        def _():
            copy(base, k - 1 + Z_SLOTS, (k - 1) % Z_SLOTS).start()

        d = row0(base, k) + lane
        own = (d >= off_col) & (d < end_col) & (d >= base + k * Z_CHUNK)
        z = _unpack_rows(zbuf[slot])
        acc_ref[...] += jnp.dot(own.astype(BF16), z, preferred_element_type=F32)
        return carry

    lax.fori_loop(0, nck, body, 0)

    @pl.when(i + 1 < n_tiles)
    def _():
        prime(i + 1)

    y = ALPHA * x_ref[...] + acc_ref[...]
    mu = jnp.mean(y, axis=-1, keepdims=True)
    yc = y - mu
    var = jnp.mean(yc * yc, axis=-1, keepdims=True)
    o_ref[...] = yc * lax.rsqrt(var + LN_EPS) * g_ref[...] + b_ref[...]


def _combine_ln(x1, z, off, cnt, g, b):
    n = x1.shape[0]
    n_rows = z.shape[0]
    T = _row_tile(n, TOKEN_TILE)
    assert n_rows >= Z_CHUNK and n_rows % ROW_ALIGN == 0
    tile_base = jnp.concatenate([off[::T], jnp.full((1,), n_rows, jnp.int32)])
    offb = jnp.broadcast_to(off[:, None], (n, LANES))
    endb = jnp.broadcast_to((off + cnt)[:, None], (n, LANES))
    row = lambda w: pl.BlockSpec((T, w), lambda i, tb: (i, 0))
    vec = pl.BlockSpec((1, D_MODEL), lambda i, tb: (0, 0))
    return pl.pallas_call(
        functools.partial(_combine_kernel, n_rows, n // T),
        grid_spec=pltpu.PrefetchScalarGridSpec(
            num_scalar_prefetch=1,
            grid=(n // T,),
            in_specs=[row(D_MODEL), row(LANES), row(LANES), vec, vec, pl.BlockSpec(memory_space=pl.ANY)],
            out_specs=row(D_MODEL),
            scratch_shapes=[pltpu.VMEM((Z_SLOTS, Z_CHUNK, PACKED), U32), pltpu.SemaphoreType.DMA((Z_SLOTS,)),
                            pltpu.VMEM((T, D_MODEL), F32)],
        ),
        out_shape=jax.ShapeDtypeStruct((n, D_MODEL), F32),
        compiler_params=_cparams(("arbitrary",)),
        name="combine_ln",
    )(tile_base, x1, offb, endb, g.reshape(1, -1), b.reshape(1, -1), z)


def _inproj1_kernel(x_ref, w_ref, tab_ref, gq_ref, gk_ref, seg_ref, q_ref, k_ref, v_ref):
    h = jnp.dot(x_ref[...].astype(BF16), w_ref[...], preferred_element_type=F32)
    sh = HEAD_DIM // 4
    seg = seg_ref[...]
    lane = lax.broadcasted_iota(jnp.int32, (1, LANES), 1)
    ones_col = jnp.where(lane == HEAD_DIM, 1.0, 0.0).astype(F32)

    def norm_rope(xc, gain, t0):
        ms = jnp.dot((xc * xc).astype(BF16), seg, preferred_element_type=F32)
        xn = xc * lax.rsqrt(ms + RMS_EPS) * gain
        return _rope_chunk(xn, tab_ref[t0], tab_ref[t0 + 1], tab_ref[t0 + 2], sh)

    for c in range(C_Q // LANES):
        xc = h[:, c * LANES:(c + 1) * LANES]
        q_ref[:, c * LANES:(c + 1) * LANES] = norm_rope(xc, gq_ref[...], 0).astype(BF16)
    for c in range(C_KV // LANES):
        xc = h[:, C_Q + c * LANES:C_Q + (c + 1) * LANES]
        kc = norm_rope(xc, gk_ref[...], 3).astype(BF16)
        k_ref[2 * c] = kc[:, :HEAD_DIM]
        k_ref[2 * c + 1] = kc[:, HEAD_DIM:]
        vc = h[:, C_Q + C_KV + c * LANES:C_Q + C_KV + (c + 1) * LANES]
        v_ref[2 * c] = jnp.where(lane < HEAD_DIM, vc, ones_col).astype(BF16)
        v_ref[2 * c + 1] = jnp.where(lane < HEAD_DIM, pltpu.roll(vc, HEAD_DIM, axis=1), ones_col).astype(BF16)


def _inproj1(x, w, tab, gq, gk, S):
    n = x.shape[0]
    tm = _row_tile(S, 512)
    nt = S // tm
    lane_head = np.arange(LANES) // HEAD_DIM
    seg = jnp.asarray((lane_head[:, None] == lane_head[None, :]) / HEAD_DIM, BF16)
    tile2 = lambda g: jnp.concatenate([g, g]).reshape(1, LANES)
    const = lambda shape: pl.BlockSpec(shape, lambda i: (0,) * len(shape))
    return pl.pallas_call(
        _inproj1_kernel,
        grid=(n // tm,),
        in_specs=[
            pl.BlockSpec((tm, D_MODEL), lambda i: (i, 0)),
            const((D_MODEL, C_IN)),
            pl.BlockSpec((6, tm, LANES), lambda i: (0, i % nt, 0)),
            const((1, LANES)), const((1, LANES)), const((LANES, LANES)),
        ],
        out_specs=[
            pl.BlockSpec((tm, C_Q), lambda i: (i, 0)),
            pl.BlockSpec((AX_KV_HEADS, tm, HEAD_DIM), lambda i: (0, i, 0)),
            pl.BlockSpec((AX_KV_HEADS, tm, LANES), lambda i: (0, i, 0)),
        ],
        out_shape=[
            jax.ShapeDtypeStruct((n, C_Q), BF16),
            jax.ShapeDtypeStruct((AX_KV_HEADS, n, HEAD_DIM), BF16),
            jax.ShapeDtypeStruct((AX_KV_HEADS, n, LANES), BF16),
        ],
        compiler_params=_cparams(("parallel",)),
        name="inproj1",
    )(x, w, tab, tile2(gq), tile2(gk), seg)


def _axattn_kernel(q_ref, k_ref, v_ref, o_ref):
    G = AX_HEADS // AX_KV_HEADS
    TQ = q_ref.shape[0]
    for r in range(TQ // AX_CHAIN_ROWS):
        rows = slice(r * AX_CHAIN_ROWS, (r + 1) * AX_CHAIN_ROWS)
        for j in range(G):
            cols = slice(j * HEAD_DIM, (j + 1) * HEAD_DIM)
            s = lax.dot_general(q_ref[rows, cols], k_ref[...], (((1,), (1,)), ((), ())),
                                preferred_element_type=F32)
            m = jnp.max(s, axis=-1, keepdims=True)
            p = jnp.exp2(s - m)
            o = jnp.dot(p.astype(BF16), v_ref[...], preferred_element_type=F32)
            o_ref[rows, cols] = (o[:, :HEAD_DIM] / o[:, HEAD_DIM:HEAD_DIM + 1]).astype(BF16)


def _axial_attention(q, k, v, B, S):
    n = q.shape[0]
    TQ = _row_tile(S, AX_Q_TILE)
    nq = S // TQ
    G = AX_HEADS // AX_KV_HEADS
    qspec = pl.BlockSpec((TQ, G * HEAD_DIM), lambda b, g, i: (b * nq + i, g))
    kspec = pl.BlockSpec((None, None, S, HEAD_DIM), lambda b, g, i: (g, b, 0, 0))
    vspec = pl.BlockSpec((None, None, S, LANES), lambda b, g, i: (g, b, 0, 0))
    return pl.pallas_call(
        _axattn_kernel,
        grid=(B, AX_KV_HEADS, nq),
        in_specs=[qspec, kspec, vspec],
        out_specs=qspec,
        out_shape=jax.ShapeDtypeStruct((n, C_Q), BF16),
        compiler_params=_cparams(("parallel", "parallel", "arbitrary")),
        name="axial_attention",
    )(q, k, v)


def _moe(x1, xb, aff, wg, wu, wd, ln_g, ln_b, n):
    cap = CAPACITY_FACTOR * n // N_EXPERTS
    gate_t, mask_t = _expert_choice(aff.T, cap)
    cnt = jnp.sum(mask_t, axis=0)
    off = jnp.cumsum(cnt) - cnt
    dest_t = off[None, :] + jnp.cumsum(mask_t, axis=0) - mask_t
    idx, gsel, dslot = _compact(mask_t, gate_t, dest_t, cap)
    xe = _gather_rows(xb, idx)
    gb = jnp.broadcast_to(gsel[:, None], (gsel.shape[0], LANES))
    ye = _expert_ffn(xe, gb, wg, wu, wd, cap)
    z = _scatter_rows(ye, dslot)
    return _combine_ln(x1, z, off, cnt, ln_g, ln_b)


def _trunk(x, p):
    B, S, _ = x.shape
    n = B * S
    xf = x.reshape(n, D_MODEL)
    tab_w = _tables_window(S)
    tab_a = _tables_axial(S)
    for layer in range(DEPTH):
        i = layer // 2
        if layer % 2 == 0:
            ug, q, k, v = _inproj0(xf, p['ab_w_in'][i], tab_w, S)
            conv = _conv_branch(ug, p['ab_conv_w'][i], p['ab_conv_b'][i], p['ab_conv_ln_g'][i],
                                p['ab_conv_ln_b'][i], S)
            att = _window_attention(q, k, v, p['ab_sink'][i], B, S)
            acts, w_out = [conv, att], p['ab_w_out'][i]
        else:
            q, k, v = _inproj1(xf, p['c_w_in'][i], tab_a, p['c_q_norm'][i], p['c_k_norm'][i], S)
            k = k.reshape(AX_KV_HEADS, B, S, HEAD_DIM)
            v = v.reshape(AX_KV_HEADS, B, S, LANES)
            acts, w_out = [_axial_attention(q, k, v, B, S)], p['c_w_out'][i]
        x1, xb, aff = _outproj_ln_router(acts, xf, w_out, p['ln1_g'][layer], p['ln1_b'][layer],
                                         p['moe_router'][layer])
        xf = _moe(x1, xb, aff, p['moe_w_gate'][layer], p['moe_w_up'][layer], p['moe_w_down'][layer],
                  p['ln2_g'][layer], p['ln2_b'][layer], n)
    return xf.reshape(B, S, D_MODEL)


def kernel(x_prompt, x_sample, ab_w_in, ab_conv_w, ab_conv_b, ab_conv_ln_g, ab_conv_ln_b, ab_sink, ab_w_out, c_w_in, c_q_norm, c_k_norm, c_w_out, ln1_g, ln1_b, moe_router, moe_w_gate, moe_w_up, moe_w_down, ln2_g, ln2_b):
    p = dict(
        ab_w_in=ab_w_in.astype(BF16), ab_conv_w=ab_conv_w, ab_conv_b=ab_conv_b,
        ab_conv_ln_g=ab_conv_ln_g, ab_conv_ln_b=ab_conv_ln_b, ab_sink=ab_sink,
        ab_w_out=ab_w_out.astype(BF16), c_w_in=c_w_in.astype(BF16), c_q_norm=c_q_norm,
        c_k_norm=c_k_norm, c_w_out=c_w_out.astype(BF16), ln1_g=ln1_g, ln1_b=ln1_b,
        moe_router=moe_router, moe_w_gate=moe_w_gate.astype(BF16), moe_w_up=moe_w_up.astype(BF16),
        moe_w_down=moe_w_down.astype(BF16), ln2_g=ln2_g, ln2_b=ln2_b,
    )
    return (_trunk(x_prompt, p), _trunk(x_sample, p))
```

```python
import functools

import jax
import jax.numpy as jnp
import numpy as np
from jax import lax
from jax.experimental import pallas as pl
from jax.experimental.pallas import tpu as pltpu
from jax.experimental.pallas import tpu_sc as plsc

D_MODEL = 1024
DEPTH = 2
GRID_W = 64
BLOCK = 128
HEAD_DIM = 64
CONV_DIM = 512
CONV_WIDTH = 31
WIN_HEADS = 8
WIN_KV_HEADS = 2
WINDOW = 128
ROT_DIM_B = 16
ROPE_THETA_B = 500000.0
AX_HEADS = 16
AX_KV_HEADS = 4
ROPE_THETA_C = 10000.0
N_EXPERTS = 16
CAPACITY_FACTOR = 2
ALPHA = (2 * DEPTH) ** 0.25
AB_Q = WIN_HEADS * HEAD_DIM
AB_KV = WIN_KV_HEADS * HEAD_DIM
AB_IN = 2 * CONV_DIM + AB_Q + 2 * AB_KV
C_Q = AX_HEADS * HEAD_DIM
C_KV = AX_KV_HEADS * HEAD_DIM
C_IN = C_Q + 2 * C_KV
LN_EPS = 1e-5
LOG2E = 1.4426950408889634
RMS_EPS = 1e-6

LANES = 128
SUBLANES = 8
CONV_ROWS = 64
CONV_SPAN = 24
MM_ROWS = 128
CONV_HALO = 16
VMEM_LIMIT = 48 * 1024 * 1024

TOKEN_TILE = 256
Z_CHUNK = 256
Z_SLOTS = 4
ROW_ALIGN = 16
SC_WINDOW = 64
COMPACT_BLOCK = 256
AX_Q_TILE = 512
AX_CHAIN_ROWS = 256

BF16 = jnp.bfloat16
F32 = jnp.float32
U32 = jnp.uint32
PACKED = D_MODEL // 2


def _pack_rows(x):
    b = lax.bitcast_convert_type(x.astype(BF16).astype(F32), U32)
    return (b[:, :PACKED] >> 16) | (b[:, PACKED:] & jnp.uint32(0xFFFF0000))


def _unpack_rows(w):
    lo = lax.bitcast_convert_type(w << 16, F32)
    hi = lax.bitcast_convert_type(w & jnp.uint32(0xFFFF0000), F32)
    return jnp.concatenate([lo, hi], axis=1).astype(BF16)


def _cparams(sem):
    return pltpu.CompilerParams(dimension_semantics=sem, vmem_limit_bytes=VMEM_LIMIT)


def _row_tile(n, pref):
    t = min(n, pref)
    assert n % t == 0
    return t


def _tables(S, groups, q_scale):
    lane = np.arange(LANES) % HEAD_DIM
    ang = jnp.zeros((S, LANES), F32)
    first = np.zeros(LANES, bool)
    second = np.zeros(LANES, bool)
    for lo, shift, theta, pos in groups:
        dim = 2 * shift
        inv = theta ** (-jnp.arange(0, dim, 2, dtype=F32) / dim)
        member = (lane >= lo) & (lane < lo + dim)
        inv_lane = jnp.where(member, inv[np.where(member, (lane - lo) % shift, 0)], 0.0)
        ang = ang + pos[:, None] * inv_lane[None, :]
        first |= member & (lane - lo < shift)
        second |= member & (lane - lo >= shift)
    cos, sin = jnp.cos(ang), jnp.sin(ang)
    sa = jnp.where(first, -sin, 0.0)
    sb = jnp.where(second, sin, 0.0)
    return jnp.stack([cos * q_scale, sa * q_scale, sb * q_scale, cos, sa, sb])


def _tables_window(S):
    pos = jnp.arange(S, dtype=F32)
    return _tables(S, [(0, ROT_DIM_B // 2, ROPE_THETA_B, pos)], HEAD_DIM ** -0.5)


def _tables_axial(S):
    rows = S // GRID_W
    gr, gc = jnp.meshgrid(jnp.arange(rows), jnp.arange(GRID_W), indexing='ij')
    groups = [(0, HEAD_DIM // 4, ROPE_THETA_C, gr.reshape(-1).astype(F32)),
              (HEAD_DIM // 2, HEAD_DIM // 4, ROPE_THETA_C, gc.reshape(-1).astype(F32))]
    return _tables(S, groups, HEAD_DIM ** -0.5 * LOG2E)


def _rope_chunk(x, c, sa, sb, shift):
    fwd = pltpu.roll(x, LANES - shift, axis=1)
    bwd = pltpu.roll(x, shift, axis=1)
    return x * c + fwd * sa + bwd * sb


def _inproj0_kernel(x_ref, w_ref, tab_ref, ug_ref, q_ref, k_ref, v_ref):
    tm = x_ref.shape[0]
    sub = min(tm, MM_ROWS)
    sh = ROT_DIM_B // 2
    q0 = 2 * CONV_DIM
    k0 = q0 + AB_Q
    for r in range(tm // sub):
        rows = slice(r * sub, (r + 1) * sub)
        tab = lambda t: tab_ref[t, rows, :]
        h = jnp.dot(x_ref[rows, :].astype(BF16), w_ref[...], preferred_element_type=F32)
        ug_ref[rows, :] = h[:, :q0].astype(BF16)
        for c in range(AB_Q // LANES):
            xc = h[:, q0 + c * LANES:q0 + (c + 1) * LANES]
            q_ref[rows, c * LANES:(c + 1) * LANES] = _rope_chunk(xc, tab(0), tab(1), tab(2), sh).astype(BF16)
        for c in range(AB_KV // LANES):
            xc = h[:, k0 + c * LANES:k0 + (c + 1) * LANES]
            k_ref[rows, c * LANES:(c + 1) * LANES] = _rope_chunk(xc, tab(3), tab(4), tab(5), sh).astype(BF16)
        v_ref[rows, :] = h[:, k0 + AB_KV:].astype(BF16)


def _inproj0(x, w, tab, S):
    n = x.shape[0]
    tm = _row_tile(S, 512)
    nt = S // tm
    return pl.pallas_call(
        _inproj0_kernel,
        grid=(n // tm,),
        in_specs=[
            pl.BlockSpec((tm, D_MODEL), lambda i: (i, 0)),
            pl.BlockSpec((D_MODEL, AB_IN), lambda i: (0, 0)),
            pl.BlockSpec((6, tm, LANES), lambda i: (0, i % nt, 0)),
        ],
        out_specs=[
            pl.BlockSpec((tm, 2 * CONV_DIM), lambda i: (i, 0)),
            pl.BlockSpec((tm, AB_Q), lambda i: (i, 0)),
            pl.BlockSpec((tm, AB_KV), lambda i: (i, 0)),
            pl.BlockSpec((tm, AB_KV), lambda i: (i, 0)),
        ],
        out_shape=[
            jax.ShapeDtypeStruct((n, 2 * CONV_DIM), BF16),
            jax.ShapeDtypeStruct((n, AB_Q), BF16),
            jax.ShapeDtypeStruct((n, AB_KV), BF16),
            jax.ShapeDtypeStruct((n, AB_KV), BF16),
        ],
        compiler_params=_cparams(("parallel",)),
        name="inproj0",
    )(x, w, tab)


def _conv_kernel(nt, cur_ref, prev_ref, next_ref, w_ref, b_ref, g_ref, beta_ref, o_ref, buf_ref, shift_ref, hc_ref):
    T = cur_ref.shape[0]
    si = pl.program_id(0) % nt

    def glu(blk):
        u = blk[:, :CONV_DIM].astype(F32)
        gt = blk[:, CONV_DIM:].astype(F32)
        return u * jax.nn.sigmoid(gt)

    pv = jnp.where(si > 0, 1.0, 0.0).astype(F32)
    nv = jnp.where(si < nt - 1, 1.0, 0.0).astype(F32)
    buf_ref[0:CONV_HALO, :] = glu(prev_ref[...]) * pv
    buf_ref[CONV_HALO:CONV_HALO + T, :] = glu(cur_ref[...])
    buf_ref[CONV_HALO + T:, :] = glu(next_ref[...]) * nv
    L = shift_ref.shape[1]
    for r in range(1, SUBLANES):
        shift_ref[r - 1] = buf_ref[r:r + L, :]

    off = CONV_HALO - CONV_WIDTH // 2

    def rows(rb, carry):
        r0 = pl.multiple_of(rb * CONV_ROWS, CONV_ROWS)
        for c in range(CONV_DIM // LANES):
            cols = slice(c * LANES, (c + 1) * LANES)
            acc = jnp.zeros((CONV_ROWS, LANES), F32)
            for r in range(SUBLANES):
                src = buf_ref if r == 0 else shift_ref.at[r - 1]
                xr = src[pl.ds(r0, CONV_ROWS + CONV_SPAN), cols]
                for a in range(CONV_SPAN // SUBLANES + 1):
                    j = SUBLANES * a + r - off
                    if 0 <= j < CONV_WIDTH:
                        wj = w_ref[j * SUBLANES:(j + 1) * SUBLANES, cols]
                        wt = jnp.concatenate([wj] * (CONV_ROWS // SUBLANES), axis=0)
                        acc = acc + xr[SUBLANES * a:SUBLANES * a + CONV_ROWS] * wt
            hc_ref[pl.ds(r0, CONV_ROWS), cols] = acc + b_ref[:, cols]
        return carry

    lax.fori_loop(0, T // CONV_ROWS, rows, 0)
    hcv = hc_ref[...]
    mu = jnp.mean(hcv, axis=-1, keepdims=True)
    xc = hcv - mu
    var = jnp.mean(xc * xc, axis=-1, keepdims=True)
    y = xc * lax.rsqrt(var + LN_EPS) * g_ref[...] + beta_ref[...]
    o_ref[...] = (y * jax.nn.sigmoid(y)).astype(BF16)


def _conv_branch(ug, w, b, g, beta, S):
    n = ug.shape[0]
    T = _row_tile(S, 512)
    nt = S // T
    r = T // CONV_HALO
    nh = n // CONV_HALO
    vec = lambda: pl.BlockSpec((1, CONV_DIM), lambda i: (0, 0))
    return pl.pallas_call(
        functools.partial(_conv_kernel, nt),
        grid=(n // T,),
        in_specs=[
            pl.BlockSpec((T, 2 * CONV_DIM), lambda i: (i, 0)),
            pl.BlockSpec((CONV_HALO, 2 * CONV_DIM), lambda i: (jnp.maximum(i * r - 1, 0), 0)),
            pl.BlockSpec((CONV_HALO, 2 * CONV_DIM), lambda i: (jnp.minimum((i + 1) * r, nh - 1), 0)),
            pl.BlockSpec((CONV_WIDTH * SUBLANES, CONV_DIM), lambda i: (0, 0)),
            vec(), vec(), vec(),
        ],
        out_specs=pl.BlockSpec((T, CONV_DIM), lambda i: (i, 0)),
        out_shape=jax.ShapeDtypeStruct((n, CONV_DIM), BF16),
        scratch_shapes=[pltpu.VMEM((T + 2 * CONV_HALO, CONV_DIM), F32),
                        pltpu.VMEM((SUBLANES - 1, T + 2 * CONV_HALO - SUBLANES, CONV_DIM), F32),
                        pltpu.VMEM((T, CONV_DIM), F32)],
        compiler_params=_cparams(("parallel",)),
        name="conv_branch",
    )(ug, ug, ug, jnp.repeat(w, SUBLANES, axis=0), b.reshape(1, -1), g.reshape(1, -1), beta.reshape(1, -1))


def _winattn_kernel(nq, sink_ref, q_ref, kp_ref, kc_ref, kn_ref, vp_ref, vc_ref, vn_ref, o_ref):
    qi = pl.program_id(1)
    G = WIN_HEADS // WIN_KV_HEADS
    M = G * BLOCK
    row = lax.broadcasted_iota(jnp.int32, (M, 3 * BLOCK), 0) % BLOCK
    col = lax.broadcasted_iota(jnp.int32, (M, 3 * BLOCK), 1)
    band = jnp.abs(col - BLOCK - row) <= WINDOW
    head = lax.broadcasted_iota(jnp.int32, (M, 1), 0) // BLOCK
    for half in range(2):
        rows = slice(half * BLOCK, (half + 1) * BLOCK)
        if half == 0:
            valid = band & ((qi > 0) | (col >= BLOCK))
            krefs, vrefs = (kp_ref, kc_ref), (vp_ref, vc_ref)
        else:
            valid = band & ((qi < nq - 1) | (col < 2 * BLOCK))
            krefs, vrefs = (kc_ref, kn_ref), (vc_ref, vn_ref)
        for g in range(WIN_KV_HEADS):
            sl = slice(g * HEAD_DIM, (g + 1) * HEAD_DIM)
            kk = jnp.concatenate([r[:, sl] for r in krefs], axis=0)
            vv = jnp.concatenate([r[:, sl] for r in vrefs], axis=0)
            heads = range(g * G, (g + 1) * G)
            qs = jnp.concatenate([q_ref[rows, h * HEAD_DIM:(h + 1) * HEAD_DIM] for h in heads], axis=0)
            s = lax.dot_general(qs, kk, (((1,), (1,)), ((), ())), preferred_element_type=F32)
            s = jnp.where(valid, s, -jnp.inf)
            sk = jnp.zeros((M, 1), F32)
            for j, h in enumerate(heads):
                sk = jnp.where(head == j, sink_ref[h], sk)
            m = jnp.maximum(jnp.max(s, axis=-1, keepdims=True), sk)
            p = jnp.exp(s - m)
            den = jnp.sum(p, axis=-1, keepdims=True) + jnp.exp(sk - m)
            o = jnp.dot(p.astype(BF16), vv, preferred_element_type=F32) / den
            for j, h in enumerate(heads):
                o_ref[rows, h * HEAD_DIM:(h + 1) * HEAD_DIM] = o[j * BLOCK:(j + 1) * BLOCK].astype(BF16)


def _window_attention(q, k, v, sink, B, S):
    n = q.shape[0]
    nb = S // BLOCK
    nq = nb // 2
    cur = lambda b, i: (b * nq + i, 0)
    prev = lambda b, i: (b * nb + jnp.maximum(2 * i - 1, 0), 0)
    nxt = lambda b, i: (b * nb + jnp.minimum(2 * i + 2, nb - 1), 0)
    edge = lambda f: pl.BlockSpec((BLOCK, AB_KV), f)
    mid = pl.BlockSpec((2 * BLOCK, AB_KV), cur)
    return pl.pallas_call(
        functools.partial(_winattn_kernel, nq),
        grid=(B, nq),
        in_specs=[
            pl.BlockSpec(memory_space=pltpu.SMEM),
            pl.BlockSpec((2 * BLOCK, AB_Q), cur),
            edge(prev), mid, edge(nxt), edge(prev), mid, edge(nxt),
        ],
        out_specs=pl.BlockSpec((2 * BLOCK, AB_Q), cur),
        out_shape=jax.ShapeDtypeStruct((n, AB_Q), BF16),
        compiler_params=_cparams(("parallel", "parallel")),
        name="window_attention",
    )(sink, q, k, k, k, v, v, v)


def _outproj_kernel(n_act, *refs):
    acts = refs[:n_act]
    x_ref, w_ref, g_ref, b_ref, rcat_ref, rhi_ref, xo_ref, xb_ref, aff_ref = refs[n_act:]
    tm = x_ref.shape[0]
    sub = min(tm, MM_ROWS)
    for r in range(tm // sub):
        rows = slice(r * sub, (r + 1) * sub)
        a = acts[0][rows, :] if n_act == 1 else jnp.concatenate([ref[rows, :] for ref in acts], axis=-1)
        mix = jnp.dot(a, w_ref[...], preferred_element_type=F32)
        y = ALPHA * x_ref[rows, :] + mix
        mu = jnp.mean(y, axis=-1, keepdims=True)
        yc = y - mu
        var = jnp.mean(yc * yc, axis=-1, keepdims=True)
        x1 = yc * lax.rsqrt(var + LN_EPS) * g_ref[...] + b_ref[...]
        xo_ref[rows, :] = x1
        xb_ref[rows, :] = _pack_rows(x1)
        x_hi = x1.astype(BF16)
        x_lo = (x1 - x_hi.astype(F32)).astype(BF16)
        t = jnp.dot(x_hi, rcat_ref[...], preferred_element_type=F32)
        logits = (t[:, :N_EXPERTS] + jnp.dot(x_lo, rhi_ref[...], preferred_element_type=F32)) + t[:, N_EXPERTS:]
        m = jnp.max(logits, axis=-1, keepdims=True)
        e = jnp.exp(logits - m)
        aff_ref[rows, :] = e / jnp.sum(e, axis=-1, keepdims=True)


def _outproj_ln_router(acts, x, w, g, b, w_r):
    n = x.shape[0]
    tm = _row_tile(n, 512)
    r_hi = w_r.astype(BF16)
    r_lo = (w_r - r_hi.astype(F32)).astype(BF16)
    row = lambda width: pl.BlockSpec((tm, width), lambda i: (i, 0))
    const = lambda shape: pl.BlockSpec(shape, lambda i: (0, 0))
    return pl.pallas_call(
        functools.partial(_outproj_kernel, len(acts)),
        grid=(n // tm,),
        in_specs=[row(a.shape[1]) for a in acts] + [
            row(D_MODEL), const((D_MODEL, D_MODEL)), const((1, D_MODEL)), const((1, D_MODEL)),
            const((D_MODEL, 2 * N_EXPERTS)), const((D_MODEL, N_EXPERTS)),
        ],
        out_specs=[row(D_MODEL), row(PACKED), row(N_EXPERTS)],
        out_shape=[
            jax.ShapeDtypeStruct((n, D_MODEL), F32),
            jax.ShapeDtypeStruct((n, PACKED), U32),
            jax.ShapeDtypeStruct((n, N_EXPERTS), F32),
        ],
        compiler_params=_cparams(("parallel",)),
        name="outproj_ln_router",
    )(*acts, x, w, g.reshape(1, -1), b.reshape(1, -1), jnp.concatenate([r_hi, r_lo], axis=1), r_hi)


def _topk_kernel(cap, aff_ref, gate_ref, mask_ref):
    n = aff_ref.shape[1]
    bits = pltpu.bitcast(aff_ref[...], jnp.int32)
    count = lambda m: jnp.sum(m.astype(jnp.int32), axis=1, keepdims=True)

    def value_step(i, t):
        cand = t | (jnp.int32(1) << (30 - i))
        return jnp.where(count(bits >= cand) >= cap, cand, t)

    thr = lax.fori_loop(0, 31, value_step, jnp.zeros((N_EXPERTS, 1), jnp.int32))
    gt = bits > thr
    eq = bits == thr
    need = cap - count(gt)
    idx = lax.broadcasted_iota(jnp.int32, (N_EXPERTS, n), 1)
    nbits = max(int(n).bit_length(), 1)

    def index_step(i, j0):
        cand = j0 | (jnp.int32(1) << (nbits - 1 - i))
        return jnp.where(count(eq & (idx < cand)) < need, cand, j0)

    j0 = lax.fori_loop(0, nbits, index_step, jnp.zeros((N_EXPERTS, 1), jnp.int32))
    sel = gt | (eq & (idx <= j0))
    gate_ref[...] = jnp.where(sel, aff_ref[...], 0.0)
    mask_ref[...] = sel.astype(jnp.int32)


def _expert_choice(aff_t, cap):
    n = aff_t.shape[1]
    return pl.pallas_call(
        functools.partial(_topk_kernel, cap),
        out_shape=[
            jax.ShapeDtypeStruct((N_EXPERTS, n), F32),
            jax.ShapeDtypeStruct((N_EXPERTS, n), jnp.int32),
        ],
        compiler_params=pltpu.CompilerParams(vmem_limit_bytes=VMEM_LIMIT),
        name="expert_choice_topk",
    )(aff_t)


def _ffn_kernel(x_ref, g_ref, wg_ref, wu_ref, wd_ref, o_ref):
    x = _unpack_rows(x_ref[...])
    hg = jnp.dot(x, wg_ref[...], preferred_element_type=F32)
    hu = jnp.dot(x, wu_ref[...], preferred_element_type=F32)
    h = (hg * jax.nn.sigmoid(hg) * hu).astype(BF16)
    y = jnp.dot(h, wd_ref[...], preferred_element_type=F32)
    o_ref[...] = _pack_rows(y * g_ref[:, 0:1])


def _expert_ffn(xe, gate, wg, wu, wd, layer, cap):
    rows = xe.shape[0]
    tm = _row_tile(cap, 1024)
    nr = cap // tm
    wspec = pl.BlockSpec((None, None, D_MODEL, D_MODEL), lambda e, r: (layer, e, 0, 0))
    return pl.pallas_call(
        _ffn_kernel,
        grid=(N_EXPERTS, nr),
        in_specs=[
            pl.BlockSpec((tm, PACKED), lambda e, r: (e * nr + r, 0)),
            pl.BlockSpec((tm, LANES), lambda e, r: (e * nr + r, 0)),
            wspec, wspec, wspec,
        ],
        out_specs=pl.BlockSpec((tm, PACKED), lambda e, r: (e * nr + r, 0)),
        out_shape=jax.ShapeDtypeStruct((rows, PACKED), U32),
        compiler_params=_cparams(("parallel", "arbitrary")),
        name="expert_ffn",
    )(xe, gate, wg, wu, wd)


def _compact_kernel(J, mask_ref, gate_ref, dest_ref, idx_ref, gsel_ref, dslot_ref):
    NC = mask_ref.shape[0]
    cap = idx_ref.shape[1]
    nt = lambda a, b: lax.dot_general(a, b, (((1,), (1,)), ((), ())), preferred_element_type=F32)
    iota = lambda shape, d: lax.broadcasted_iota(jnp.int32, shape, d)
    m = mask_ref[...].astype(F32).astype(BF16)
    tril = (iota((LANES, LANES), 1) <= iota((LANES, LANES), 0)).astype(BF16)
    eye = (iota((LANES, LANES), 1) == iota((LANES, LANES), 0)).astype(BF16)
    lt = nt(tril, m).astype(BF16)
    tot_b = jnp.dot(m, jnp.ones((LANES, LANES), BF16), preferred_element_type=F32)
    trilc = (iota((NC, NC), 1) <= iota((NC, NC), 0)).astype(BF16)
    cum_col = jnp.dot(trilc, tot_b.astype(BF16), preferred_element_type=F32)[:, 0:1]
    tot_col = tot_b[:, 0:1]
    g = gate_ref[...]
    g0 = g.astype(BF16)
    r1 = g - g0.astype(F32)
    g1 = r1.astype(BF16)
    g2 = (r1 - g1.astype(F32)).astype(BF16)
    gts = [nt(eye, p).astype(BF16) for p in (g0, g1, g2)]
    d = dest_ref[...]
    dts = [nt(eye, p.astype(F32).astype(BF16)).astype(BF16)
           for p in (d & 255, (d >> 8) & 255, d >> 16)]
    chunk_id = iota((NC, J), 0).astype(F32)
    tok_id = iota((LANES, J), 0).astype(F32)
    colsum = lambda a: jnp.sum(a, axis=0, keepdims=True)

    def block(jb, carry):
        j0 = pl.multiple_of(jb * J, J)
        slot = (j0 + iota((1, J), 1)).astype(F32)
        before = cum_col <= slot
        c_j = colsum(before.astype(F32))
        rank = slot - colsum(jnp.where(before, tot_col, 0.0))
        onehot = (chunk_id == c_j).astype(BF16)
        within = colsum((jnp.dot(lt, onehot, preferred_element_type=F32) <= rank).astype(F32))
        tsel = tok_id == within
        pick = lambda a: colsum(jnp.where(tsel, jnp.dot(a, onehot, preferred_element_type=F32), 0.0))
        idx_ref[:, pl.ds(j0, J)] = (c_j * LANES + within).astype(jnp.int32)
        gsel_ref[:, pl.ds(j0, J)] = (pick(gts[0]) + pick(gts[1])) + pick(gts[2])
        dslot_ref[:, pl.ds(j0, J)] = (pick(dts[0]) + 256.0 * pick(dts[1]) + 65536.0 * pick(dts[2])).astype(jnp.int32)
        return carry

    lax.fori_loop(0, cap // J, block, 0)


def _compact(mask_t, gate_t, dest_t, cap):
    n = mask_t.shape[1]
    nc = n // LANES
    J = _row_tile(cap, COMPACT_BLOCK)
    chunked = lambda a: a.reshape(N_EXPERTS, nc, LANES)
    ispec = pl.BlockSpec((None, nc, LANES), lambda e: (e, 0, 0))
    ospec = pl.BlockSpec((None, 1, cap), lambda e: (e, 0, 0))
    oshape = lambda dt: jax.ShapeDtypeStruct((N_EXPERTS, 1, cap), dt)
    idx, gsel, dslot = pl.pallas_call(
        functools.partial(_compact_kernel, J),
        grid=(N_EXPERTS,),
        in_specs=[ispec, ispec, ispec],
        out_specs=[ospec, ospec, ospec],
        out_shape=[oshape(jnp.int32), oshape(F32), oshape(jnp.int32)],
        compiler_params=_cparams(("parallel",)),
        name="slot_lists",
    )(chunked(mask_t), chunked(gate_t), chunked(dest_t))
    return idx.reshape(-1), gsel.reshape(-1), dslot.reshape(-1)


def _sc_rows(data, idx, gather):
    R = idx.shape[0]
    D = data.shape[1]
    info = plsc.get_sparse_core_info()
    nw = info.num_cores * info.num_subcores
    per_w = R // nw
    assert per_w * nw == R and per_w % SC_WINDOW == 0
    mesh = plsc.VectorSubcoreMesh(core_axis_name="c", subcore_axis_name="s")

    @functools.partial(
        pl.kernel, mesh=mesh,
        out_type=jax.ShapeDtypeStruct((R, D), data.dtype),
        scratch_types=[pltpu.VMEM((SC_WINDOW,), jnp.int32),
                       pltpu.VMEM((SC_WINDOW, D), data.dtype),
                       pltpu.SemaphoreType.DMA],
        name="sc_gather_rows" if gather else "sc_scatter_rows",
    )
    def move(data_hbm, idx_hbm, out_hbm, idx_v, rows_v, sem):
        wid = lax.axis_index("s") * info.num_cores + lax.axis_index("c")

        @pl.loop(0, per_w // SC_WINDOW)
        def _(j):
            base = wid * per_w + j * SC_WINDOW
            pltpu.sync_copy(idx_hbm.at[pl.ds(base, SC_WINDOW)], idx_v)
            if gather:
                pltpu.async_copy(data_hbm.at[idx_v], rows_v, sem).wait()
                pltpu.sync_copy(rows_v, out_hbm.at[pl.ds(base, SC_WINDOW)])
            else:
                pltpu.sync_copy(data_hbm.at[pl.ds(base, SC_WINDOW)], rows_v)
                pltpu.async_copy(rows_v, out_hbm.at[idx_v], sem).wait()

    return move(data, idx)


def _gather_rows(table, idx):
    return _sc_rows(table, idx, True)


def _scatter_rows(src, idx):
    return _sc_rows(src, idx, False)


def _combine_kernel(n_rows, n_tiles, base_ref, x_ref, off_ref, end_ref, g_ref, b_ref, z_hbm, o_ref,
                    zbuf, sem, acc_ref):
    i = pl.program_id(0)

    def span(t):
        b0 = (base_ref[t] // ROW_ALIGN) * ROW_ALIGN
        return b0, (base_ref[t + 1] - b0 + Z_CHUNK - 1) // Z_CHUNK

    def row0(b0, k):
        return pl.multiple_of(jnp.minimum(b0 + k * Z_CHUNK, n_rows - Z_CHUNK), ROW_ALIGN)

    def copy(t, b0, k):
        slot = (t % 2) * Z_SLOTS + k % Z_SLOTS
        return pltpu.make_async_copy(z_hbm.at[pl.ds(row0(b0, k), Z_CHUNK)], zbuf.at[slot], sem.at[slot])

    def prime(t):
        b0, cnt = span(t)
        for s in range(Z_SLOTS):
            @pl.when(s < cnt)
            def _():
                copy(t, b0, s).start()

    @pl.when(i == 0)
    def _():
        prime(0)

    @pl.when(i + 1 < n_tiles)
    def _():
        prime(i + 1)

    base, nck = span(i)
    acc_ref[...] = jnp.zeros_like(acc_ref)
    off_col = off_ref[:, 0:1]
    end_col = end_ref[:, 0:1]
    lane = lax.broadcasted_iota(jnp.int32, (1, Z_CHUNK), 1)

    def body(k, carry):
        copy(i, base, k).wait()

        @pl.when((k >= 1) & (k - 1 + Z_SLOTS < nck))
        def _():
            copy(i, base, k - 1 + Z_SLOTS).start()

        d = row0(base, k) + lane
        own = (d >= off_col) & (d < end_col) & (d >= base + k * Z_CHUNK)
        z = _unpack_rows(zbuf[(i % 2) * Z_SLOTS + k % Z_SLOTS])
        acc_ref[...] += jnp.dot(own.astype(BF16), z, preferred_element_type=F32)
        return carry

    lax.fori_loop(0, nck, body, 0)

    y = ALPHA * x_ref[...] + acc_ref[...]
    mu = jnp.mean(y, axis=-1, keepdims=True)
    yc = y - mu
    var = jnp.mean(yc * yc, axis=-1, keepdims=True)
    o_ref[...] = yc * lax.rsqrt(var + LN_EPS) * g_ref[...] + b_ref[...]


def _combine_ln(x1, z, off, cnt, g, b):
    n = x1.shape[0]
    n_rows = z.shape[0]
    T = _row_tile(n, TOKEN_TILE)
    assert n_rows >= Z_CHUNK and n_rows % ROW_ALIGN == 0
    tile_base = jnp.concatenate([off[::T], jnp.full((1,), n_rows, jnp.int32)])
    offb = jnp.broadcast_to(off[:, None], (n, LANES))
    endb = jnp.broadcast_to((off + cnt)[:, None], (n, LANES))
    row = lambda w: pl.BlockSpec((T, w), lambda i, tb: (i, 0))
    vec = pl.BlockSpec((1, D_MODEL), lambda i, tb: (0, 0))
    return pl.pallas_call(
        functools.partial(_combine_kernel, n_rows, n // T),
        grid_spec=pltpu.PrefetchScalarGridSpec(
            num_scalar_prefetch=1,
            grid=(n // T,),
            in_specs=[row(D_MODEL), row(LANES), row(LANES), vec, vec, pl.BlockSpec(memory_space=pl.ANY)],
            out_specs=row(D_MODEL),
            scratch_shapes=[pltpu.VMEM((2 * Z_SLOTS, Z_CHUNK, PACKED), U32),
                            pltpu.SemaphoreType.DMA((2 * Z_SLOTS,)),
                            pltpu.VMEM((T, D_MODEL), F32)],
        ),
        out_shape=jax.ShapeDtypeStruct((n, D_MODEL), F32),
        compiler_params=_cparams(("arbitrary",)),
        name="combine_ln",
    )(tile_base, x1, offb, endb, g.reshape(1, -1), b.reshape(1, -1), z)


def _inproj1_kernel(x_ref, w_ref, tab_ref, gq_ref, gk_ref, seg_ref, q_ref, k_ref, v_ref):
    tm = x_ref.shape[0]
    sub = min(tm, MM_ROWS)
    sh = HEAD_DIM // 4
    seg = seg_ref[...]
    lane = lax.broadcasted_iota(jnp.int32, (1, LANES), 1)
    ones_col = jnp.where(lane == HEAD_DIM, 1.0, 0.0).astype(F32)
    for r in range(tm // sub):
        rows = slice(r * sub, (r + 1) * sub)
        h = jnp.dot(x_ref[rows, :].astype(BF16), w_ref[...], preferred_element_type=F32)

        def norm_rope(xc, gain, t0):
            ms = jnp.dot((xc * xc).astype(BF16), seg, preferred_element_type=F32)
            xn = xc * lax.rsqrt(ms + RMS_EPS) * gain
            return _rope_chunk(xn, tab_ref[t0, rows, :], tab_ref[t0 + 1, rows, :], tab_ref[t0 + 2, rows, :], sh)

        for c in range(C_Q // LANES):
            xc = h[:, c * LANES:(c + 1) * LANES]
            q_ref[rows, c * LANES:(c + 1) * LANES] = norm_rope(xc, gq_ref[...], 0).astype(BF16)
        for c in range(C_KV // LANES):
            xc = h[:, C_Q + c * LANES:C_Q + (c + 1) * LANES]
            kc = norm_rope(xc, gk_ref[...], 3).astype(BF16)
            k_ref[2 * c, rows, :] = kc[:, :HEAD_DIM]
            k_ref[2 * c + 1, rows, :] = kc[:, HEAD_DIM:]
            vc = h[:, C_Q + C_KV + c * LANES:C_Q + C_KV + (c + 1) * LANES]
            v_ref[2 * c, rows, :] = jnp.where(lane < HEAD_DIM, vc, ones_col).astype(BF16)
            v_ref[2 * c + 1, rows, :] = jnp.where(lane < HEAD_DIM, pltpu.roll(vc, HEAD_DIM, axis=1),
                                                  ones_col).astype(BF16)


def _inproj1(x, w, tab, gq, gk, S):
    n = x.shape[0]
    tm = _row_tile(S, 512)
    nt = S // tm
    lane_head = np.arange(LANES) // HEAD_DIM
    seg = jnp.asarray((lane_head[:, None] == lane_head[None, :]) / HEAD_DIM, BF16)
    tile2 = lambda g: jnp.concatenate([g, g]).reshape(1, LANES)
    const = lambda shape: pl.BlockSpec(shape, lambda i: (0,) * len(shape))
    return pl.pallas_call(
        _inproj1_kernel,
        grid=(n // tm,),
        in_specs=[
            pl.BlockSpec((tm, D_MODEL), lambda i: (i, 0)),
            const((D_MODEL, C_IN)),
            pl.BlockSpec((6, tm, LANES), lambda i: (0, i % nt, 0)),
            const((1, LANES)), const((1, LANES)), const((LANES, LANES)),
        ],
        out_specs=[
            pl.BlockSpec((tm, C_Q), lambda i: (i, 0)),
            pl.BlockSpec((AX_KV_HEADS, tm, HEAD_DIM), lambda i: (0, i, 0)),
            pl.BlockSpec((AX_KV_HEADS, tm, LANES), lambda i: (0, i, 0)),
        ],
        out_shape=[
            jax.ShapeDtypeStruct((n, C_Q), BF16),
            jax.ShapeDtypeStruct((AX_KV_HEADS, n, HEAD_DIM), BF16),
            jax.ShapeDtypeStruct((AX_KV_HEADS, n, LANES), BF16),
        ],
        compiler_params=_cparams(("parallel",)),
        name="inproj1",
    )(x, w, tab, tile2(gq), tile2(gk), seg)


def _axattn_kernel(q_ref, k_ref, v_ref, o_ref):
    G = AX_HEADS // AX_KV_HEADS
    TQ = q_ref.shape[0]
    for r in range(TQ // AX_CHAIN_ROWS):
        rows = slice(r * AX_CHAIN_ROWS, (r + 1) * AX_CHAIN_ROWS)
        for j in range(G):
            cols = slice(j * HEAD_DIM, (j + 1) * HEAD_DIM)
            s = lax.dot_general(q_ref[rows, cols], k_ref[...], (((1,), (1,)), ((), ())),
                                preferred_element_type=F32)
            m = jnp.max(s, axis=-1, keepdims=True)
            p = jnp.exp2(s - m)
            o = jnp.dot(p.astype(BF16), v_ref[...], preferred_element_type=F32)
            o_ref[rows, cols] = (o[:, :HEAD_DIM] / o[:, HEAD_DIM:HEAD_DIM + 1]).astype(BF16)


def _axial_attention(q, k, v, B, S):
    n = q.shape[0]
    TQ = _row_tile(S, AX_Q_TILE)
    nq = S // TQ
    G = AX_HEADS // AX_KV_HEADS
    qspec = pl.BlockSpec((TQ, G * HEAD_DIM), lambda b, g, i: (b * nq + i, g))
    kspec = pl.BlockSpec((None, None, S, HEAD_DIM), lambda b, g, i: (g, b, 0, 0))
    vspec = pl.BlockSpec((None, None, S, LANES), lambda b, g, i: (g, b, 0, 0))
    return pl.pallas_call(
        _axattn_kernel,
        grid=(B, AX_KV_HEADS, nq),
        in_specs=[qspec, kspec, vspec],
        out_specs=qspec,
        out_shape=jax.ShapeDtypeStruct((n, C_Q), BF16),
        compiler_params=_cparams(("parallel", "parallel", "arbitrary")),
        name="axial_attention",
    )(q, k, v)


def _moe(x1, xb, aff, wg, wu, wd, layer, ln_g, ln_b, n):
    cap = CAPACITY_FACTOR * n // N_EXPERTS
    gate_t, mask_t = _expert_choice(aff.T, cap)
    cnt = jnp.sum(mask_t, axis=0)
    off = jnp.cumsum(cnt) - cnt
    dest_t = off[None, :] + jnp.cumsum(mask_t, axis=0) - mask_t
    idx, gsel, dslot = _compact(mask_t, gate_t, dest_t, cap)
    xe = _gather_rows(xb, idx)
    gb = jnp.broadcast_to(gsel[:, None], (gsel.shape[0], LANES))
    ye = _expert_ffn(xe, gb, wg, wu, wd, layer, cap)
    z = _scatter_rows(ye, dslot)
    return _combine_ln(x1, z, off, cnt, ln_g, ln_b)


def _trunk(x, p):
    B, S, _ = x.shape
    n = B * S
    xf = x.reshape(n, D_MODEL)
    tab_w = _tables_window(S)
    tab_a = _tables_axial(S)
    for layer in range(DEPTH):
        i = layer // 2
        if layer % 2 == 0:
            ug, q, k, v = _inproj0(xf, p['ab_w_in'][i], tab_w, S)
            conv = _conv_branch(ug, p['ab_conv_w'][i], p['ab_conv_b'][i], p['ab_conv_ln_g'][i],
                                p['ab_conv_ln_b'][i], S)
            att = _window_attention(q, k, v, p['ab_sink'][i], B, S)
            acts, w_out = [conv, att], p['ab_w_out'][i]
        else:
            q, k, v = _inproj1(xf, p['c_w_in'][i], tab_a, p['c_q_norm'][i], p['c_k_norm'][i], S)
            k = k.reshape(AX_KV_HEADS, B, S, HEAD_DIM)
            v = v.reshape(AX_KV_HEADS, B, S, LANES)
            acts, w_out = [_axial_attention(q, k, v, B, S)], p['c_w_out'][i]
        x1, xb, aff = _outproj_ln_router(acts, xf, w_out, p['ln1_g'][layer], p['ln1_b'][layer],
                                         p['moe_router'][layer])
        xf = _moe(x1, xb, aff, p['moe_w_gate'], p['moe_w_up'], p['moe_w_down'], layer,
                  p['ln2_g'][layer], p['ln2_b'][layer], n)
    return xf.reshape(B, S, D_MODEL)


def kernel(x_prompt, x_sample, ab_w_in, ab_conv_w, ab_conv_b, ab_conv_ln_g, ab_conv_ln_b, ab_sink, ab_w_out, c_w_in, c_q_norm, c_k_norm, c_w_out, ln1_g, ln1_b, moe_router, moe_w_gate, moe_w_up, moe_w_down, ln2_g, ln2_b):
    p = dict(
        ab_w_in=ab_w_in.astype(BF16), ab_conv_w=ab_conv_w, ab_conv_b=ab_conv_b,
        ab_conv_ln_g=ab_conv_ln_g, ab_conv_ln_b=ab_conv_ln_b, ab_sink=ab_sink,
        ab_w_out=ab_w_out.astype(BF16), c_w_in=c_w_in.astype(BF16), c_q_norm=c_q_norm,
        c_k_norm=c_k_norm, c_w_out=c_w_out.astype(BF16), ln1_g=ln1_g, ln1_b=ln1_b,
        moe_router=moe_router, moe_w_gate=moe_w_gate.astype(BF16), moe_w_up=moe_w_up.astype(BF16),
        moe_w_down=moe_w_down.astype(BF16), ln2_g=ln2_g, ln2_b=ln2_b,
    )
    return (_trunk(x_prompt, p), _trunk(x_sample, p))
```

```python
import functools

import jax
import jax.numpy as jnp
import numpy as np
from jax import lax
from jax.experimental import pallas as pl
from jax.experimental.pallas import tpu as pltpu
from jax.experimental.pallas import tpu_sc as plsc

D_MODEL = 1024
DEPTH = 2
GRID_W = 64
BLOCK = 128
HEAD_DIM = 64
CONV_DIM = 512
CONV_WIDTH = 31
WIN_HEADS = 8
WIN_KV_HEADS = 2
WINDOW = 128
ROT_DIM_B = 16
ROPE_THETA_B = 500000.0
AX_HEADS = 16
AX_KV_HEADS = 4
ROPE_THETA_C = 10000.0
N_EXPERTS = 16
CAPACITY_FACTOR = 2
ALPHA = (2 * DEPTH) ** 0.25
AB_Q = WIN_HEADS * HEAD_DIM
AB_KV = WIN_KV_HEADS * HEAD_DIM
AB_IN = 2 * CONV_DIM + AB_Q + 2 * AB_KV
C_Q = AX_HEADS * HEAD_DIM
C_KV = AX_KV_HEADS * HEAD_DIM
C_IN = C_Q + 2 * C_KV
LN_EPS = 1e-5
LOG2E = 1.4426950408889634
RMS_EPS = 1e-6

LANES = 128
SUBLANES = 8
CONV_ROWS = 64
CONV_SPAN = 24
MM_ROWS = 128
CONV_HALO = 16
VMEM_LIMIT = 48 * 1024 * 1024

TOKEN_TILE = 256
Z_CHUNK = 256
Z_SLOTS = 4
ROW_ALIGN = 16
SC_WINDOW = 64
COMPACT_BLOCK = 256
AX_Q_TILE = 512
AX_CHAIN_ROWS = 256

BF16 = jnp.bfloat16
F32 = jnp.float32
U32 = jnp.uint32
PACKED = D_MODEL // 2


def _pack_rows(x):
    b = lax.bitcast_convert_type(x.astype(BF16).astype(F32), U32)
    return (b[:, :PACKED] >> 16) | (b[:, PACKED:] & jnp.uint32(0xFFFF0000))


def _unpack_rows(w):
    lo = lax.bitcast_convert_type(w << 16, F32)
    hi = lax.bitcast_convert_type(w & jnp.uint32(0xFFFF0000), F32)
    return jnp.concatenate([lo, hi], axis=1).astype(BF16)


def _cparams(sem):
    return pltpu.CompilerParams(dimension_semantics=sem, vmem_limit_bytes=VMEM_LIMIT)


def _row_tile(n, pref):
    t = min(n, pref)
    assert n % t == 0
    return t


def _tables(S, groups, q_scale):
    lane = np.arange(LANES) % HEAD_DIM
    ang = jnp.zeros((S, LANES), F32)
    first = np.zeros(LANES, bool)
    second = np.zeros(LANES, bool)
    for lo, shift, theta, pos in groups:
        dim = 2 * shift
        inv = theta ** (-jnp.arange(0, dim, 2, dtype=F32) / dim)
        member = (lane >= lo) & (lane < lo + dim)
        inv_lane = jnp.where(member, inv[np.where(member, (lane - lo) % shift, 0)], 0.0)
        ang = ang + pos[:, None] * inv_lane[None, :]
        first |= member & (lane - lo < shift)
        second |= member & (lane - lo >= shift)
    cos, sin = jnp.cos(ang), jnp.sin(ang)
    sa = jnp.where(first, -sin, 0.0)
    sb = jnp.where(second, sin, 0.0)
    return jnp.stack([cos * q_scale, sa * q_scale, sb * q_scale, cos, sa, sb])


def _tables_window(S):
    pos = jnp.arange(S, dtype=F32)
    return _tables(S, [(0, ROT_DIM_B // 2, ROPE_THETA_B, pos)], HEAD_DIM ** -0.5)


def _tables_axial(S):
    rows = S // GRID_W
    gr, gc = jnp.meshgrid(jnp.arange(rows), jnp.arange(GRID_W), indexing='ij')
    groups = [(0, HEAD_DIM // 4, ROPE_THETA_C, gr.reshape(-1).astype(F32)),
              (HEAD_DIM // 2, HEAD_DIM // 4, ROPE_THETA_C, gc.reshape(-1).astype(F32))]
    return _tables(S, groups, HEAD_DIM ** -0.5 * LOG2E)


def _rope_chunk(x, c, sa, sb, shift):
    fwd = pltpu.roll(x, LANES - shift, axis=1)
    bwd = pltpu.roll(x, shift, axis=1)
    return x * c + fwd * sa + bwd * sb


def _inproj0_kernel(x_ref, w_ref, tab_ref, ug_ref, q_ref, k_ref, v_ref):
    tm = x_ref.shape[0]
    sub = min(tm, MM_ROWS)
    sh = ROT_DIM_B // 2
    q0 = 2 * CONV_DIM
    k0 = q0 + AB_Q
    for r in range(tm // sub):
        rows = slice(r * sub, (r + 1) * sub)
        tab = lambda t: tab_ref[t, rows, :]
        h = jnp.dot(x_ref[rows, :].astype(BF16), w_ref[...], preferred_element_type=F32)
        ug_ref[rows, :] = h[:, :q0].astype(BF16)
        for c in range(AB_Q // LANES):
            xc = h[:, q0 + c * LANES:q0 + (c + 1) * LANES]
            q_ref[rows, c * LANES:(c + 1) * LANES] = _rope_chunk(xc, tab(0), tab(1), tab(2), sh).astype(BF16)
        for c in range(AB_KV // LANES):
            xc = h[:, k0 + c * LANES:k0 + (c + 1) * LANES]
            k_ref[rows, c * LANES:(c + 1) * LANES] = _rope_chunk(xc, tab(3), tab(4), tab(5), sh).astype(BF16)
        v_ref[rows, :] = h[:, k0 + AB_KV:].astype(BF16)


def _inproj0(x, w, tab, S):
    n = x.shape[0]
    tm = _row_tile(S, 512)
    nt = S // tm
    return pl.pallas_call(
        _inproj0_kernel,
        grid=(n // tm,),
        in_specs=[
            pl.BlockSpec((tm, D_MODEL), lambda i: (i, 0)),
            pl.BlockSpec((D_MODEL, AB_IN), lambda i: (0, 0)),
            pl.BlockSpec((6, tm, LANES), lambda i: (0, i % nt, 0)),
        ],
        out_specs=[
            pl.BlockSpec((tm, 2 * CONV_DIM), lambda i: (i, 0)),
            pl.BlockSpec((tm, AB_Q), lambda i: (i, 0)),
            pl.BlockSpec((tm, AB_KV), lambda i: (i, 0)),
            pl.BlockSpec((tm, AB_KV), lambda i: (i, 0)),
        ],
        out_shape=[
            jax.ShapeDtypeStruct((n, 2 * CONV_DIM), BF16),
            jax.ShapeDtypeStruct((n, AB_Q), BF16),
            jax.ShapeDtypeStruct((n, AB_KV), BF16),
            jax.ShapeDtypeStruct((n, AB_KV), BF16),
        ],
        compiler_params=_cparams(("parallel",)),
        name="inproj0",
    )(x, w, tab)


def _conv_kernel(nt, cur_ref, prev_ref, next_ref, w_ref, b_ref, g_ref, beta_ref, o_ref, buf_ref, shift_ref, hc_ref):
    T = cur_ref.shape[0]
    si = pl.program_id(0) % nt

    def glu(blk):
        u = blk[:, :CONV_DIM].astype(F32)
        gt = blk[:, CONV_DIM:].astype(F32)
        return u * jax.nn.sigmoid(gt)

    pv = jnp.where(si > 0, 1.0, 0.0).astype(F32)
    nv = jnp.where(si < nt - 1, 1.0, 0.0).astype(F32)
    buf_ref[0:CONV_HALO, :] = glu(prev_ref[...]) * pv
    buf_ref[CONV_HALO:CONV_HALO + T, :] = glu(cur_ref[...])
    buf_ref[CONV_HALO + T:, :] = glu(next_ref[...]) * nv
    L = shift_ref.shape[1]
    for r in range(1, SUBLANES):
        shift_ref[r - 1] = buf_ref[r:r + L, :]

    off = CONV_HALO - CONV_WIDTH // 2

    def rows(rb, carry):
        r0 = pl.multiple_of(rb * CONV_ROWS, CONV_ROWS)
        for c in range(CONV_DIM // LANES):
            cols = slice(c * LANES, (c + 1) * LANES)
            acc = jnp.zeros((CONV_ROWS, LANES), F32)
            for r in range(SUBLANES):
                src = buf_ref if r == 0 else shift_ref.at[r - 1]
                xr = src[pl.ds(r0, CONV_ROWS + CONV_SPAN), cols]
                for a in range(CONV_SPAN // SUBLANES + 1):
                    j = SUBLANES * a + r - off
                    if 0 <= j < CONV_WIDTH:
                        wj = w_ref[j * SUBLANES:(j + 1) * SUBLANES, cols]
                        wt = jnp.concatenate([wj] * (CONV_ROWS // SUBLANES), axis=0)
                        acc = acc + xr[SUBLANES * a:SUBLANES * a + CONV_ROWS] * wt
            hc_ref[pl.ds(r0, CONV_ROWS), cols] = acc + b_ref[:, cols]
        return carry

    lax.fori_loop(0, T // CONV_ROWS, rows, 0)
    hcv = hc_ref[...]
    mu = jnp.mean(hcv, axis=-1, keepdims=True)
    xc = hcv - mu
    var = jnp.mean(xc * xc, axis=-1, keepdims=True)
    y = xc * lax.rsqrt(var + LN_EPS) * g_ref[...] + beta_ref[...]
    o_ref[...] = (y * jax.nn.sigmoid(y)).astype(BF16)


def _conv_branch(ug, w, b, g, beta, S):
    n = ug.shape[0]
    T = _row_tile(S, 512)
    nt = S // T
    r = T // CONV_HALO
    nh = n // CONV_HALO
    vec = lambda: pl.BlockSpec((1, CONV_DIM), lambda i: (0, 0))
    return pl.pallas_call(
        functools.partial(_conv_kernel, nt),
        grid=(n // T,),
        in_specs=[
            pl.BlockSpec((T, 2 * CONV_DIM), lambda i: (i, 0)),
            pl.BlockSpec((CONV_HALO, 2 * CONV_DIM), lambda i: (jnp.maximum(i * r - 1, 0), 0)),
            pl.BlockSpec((CONV_HALO, 2 * CONV_DIM), lambda i: (jnp.minimum((i + 1) * r, nh - 1), 0)),
            pl.BlockSpec((CONV_WIDTH * SUBLANES, CONV_DIM), lambda i: (0, 0)),
            vec(), vec(), vec(),
        ],
        out_specs=pl.BlockSpec((T, CONV_DIM), lambda i: (i, 0)),
        out_shape=jax.ShapeDtypeStruct((n, CONV_DIM), BF16),
        scratch_shapes=[pltpu.VMEM((T + 2 * CONV_HALO, CONV_DIM), F32),
                        pltpu.VMEM((SUBLANES - 1, T + 2 * CONV_HALO - SUBLANES, CONV_DIM), F32),
                        pltpu.VMEM((T, CONV_DIM), F32)],
        compiler_params=_cparams(("parallel",)),
        name="conv_branch",
    )(ug, ug, ug, jnp.repeat(w, SUBLANES, axis=0), b.reshape(1, -1), g.reshape(1, -1), beta.reshape(1, -1))


def _winattn_kernel(nq, sink_ref, q_ref, kp_ref, kc_ref, kn_ref, vp_ref, vc_ref, vn_ref, o_ref):
    qi = pl.program_id(1)
    G = WIN_HEADS // WIN_KV_HEADS
    M = G * BLOCK
    row = lax.broadcasted_iota(jnp.int32, (M, 3 * BLOCK), 0) % BLOCK
    col = lax.broadcasted_iota(jnp.int32, (M, 3 * BLOCK), 1)
    band = jnp.abs(col - BLOCK - row) <= WINDOW
    head = lax.broadcasted_iota(jnp.int32, (M, 1), 0) // BLOCK
    chains = [(half, g) for half in range(2) for g in range(WIN_KV_HEADS)]

    def operands(half, g):
        sl = slice(g * HEAD_DIM, (g + 1) * HEAD_DIM)
        krefs, vrefs = ((kp_ref, kc_ref), (vp_ref, vc_ref)) if half == 0 else ((kc_ref, kn_ref), (vc_ref, vn_ref))
        return (jnp.concatenate([r[:, sl] for r in krefs], axis=0),
                jnp.concatenate([r[:, sl] for r in vrefs], axis=0))

    def scores(half, g):
        rows = slice(half * BLOCK, (half + 1) * BLOCK)
        qs = jnp.concatenate([q_ref[rows, h * HEAD_DIM:(h + 1) * HEAD_DIM] for h in range(g * G, (g + 1) * G)],
                             axis=0)
        return lax.dot_general(qs, operands(half, g)[0], (((1,), (1,)), ((), ())), preferred_element_type=F32)

    s_next = scores(*chains[0])
    for c, (half, g) in enumerate(chains):
        rows = slice(half * BLOCK, (half + 1) * BLOCK)
        s = s_next
        if c + 1 < len(chains):
            s_next = scores(*chains[c + 1])
        if half == 0:
            valid = band & ((qi > 0) | (col >= BLOCK))
        else:
            valid = band & ((qi < nq - 1) | (col < 2 * BLOCK))
        heads = range(g * G, (g + 1) * G)
        s = jnp.where(valid, s, -jnp.inf)
        sk = jnp.zeros((M, 1), F32)
        for j, h in enumerate(heads):
            sk = jnp.where(head == j, sink_ref[h], sk)
        m = jnp.maximum(jnp.max(s, axis=-1, keepdims=True), sk)
        p = jnp.exp(s - m)
        den = jnp.sum(p, axis=-1, keepdims=True) + jnp.exp(sk - m)
        o = jnp.dot(p.astype(BF16), operands(half, g)[1], preferred_element_type=F32) / den
        for j, h in enumerate(heads):
            o_ref[rows, h * HEAD_DIM:(h + 1) * HEAD_DIM] = o[j * BLOCK:(j + 1) * BLOCK].astype(BF16)


def _window_attention(q, k, v, sink, B, S):
    n = q.shape[0]
    nb = S // BLOCK
    nq = nb // 2
    cur = lambda b, i: (b * nq + i, 0)
    prev = lambda b, i: (b * nb + jnp.maximum(2 * i - 1, 0), 0)
    nxt = lambda b, i: (b * nb + jnp.minimum(2 * i + 2, nb - 1), 0)
    edge = lambda f: pl.BlockSpec((BLOCK, AB_KV), f)
    mid = pl.BlockSpec((2 * BLOCK, AB_KV), cur)
    return pl.pallas_call(
        functools.partial(_winattn_kernel, nq),
        grid=(B, nq),
        in_specs=[
            pl.BlockSpec(memory_space=pltpu.SMEM),
            pl.BlockSpec((2 * BLOCK, AB_Q), cur),
            edge(prev), mid, edge(nxt), edge(prev), mid, edge(nxt),
        ],
        out_specs=pl.BlockSpec((2 * BLOCK, AB_Q), cur),
        out_shape=jax.ShapeDtypeStruct((n, AB_Q), BF16),
        compiler_params=_cparams(("parallel", "parallel")),
        name="window_attention",
    )(sink, q, k, k, k, v, v, v)


def _outproj_kernel(n_act, *refs):
    acts = refs[:n_act]
    x_ref, w_ref, g_ref, b_ref, rcat_ref, rhi_ref, xo_ref, xb_ref, aff_ref = refs[n_act:]
    tm = x_ref.shape[0]
    sub = min(tm, MM_ROWS)
    def project(r):
        rows = slice(r * sub, (r + 1) * sub)
        a = acts[0][rows, :] if n_act == 1 else jnp.concatenate([ref[rows, :] for ref in acts], axis=-1)
        return jnp.dot(a, w_ref[...], preferred_element_type=F32)

    mix_next = project(0)
    for r in range(tm // sub):
        rows = slice(r * sub, (r + 1) * sub)
        mix = mix_next
        if r + 1 < tm // sub:
            mix_next = project(r + 1)
        y = ALPHA * x_ref[rows, :] + mix
        mu = jnp.mean(y, axis=-1, keepdims=True)
        yc = y - mu
        var = jnp.mean(yc * yc, axis=-1, keepdims=True)
        x1 = yc * lax.rsqrt(var + LN_EPS) * g_ref[...] + b_ref[...]
        xo_ref[rows, :] = x1
        xb_ref[rows, :] = _pack_rows(x1)
        x_hi = x1.astype(BF16)
        x_lo = (x1 - x_hi.astype(F32)).astype(BF16)
        t = jnp.dot(x_hi, rcat_ref[...], preferred_element_type=F32)
        logits = (t[:, :N_EXPERTS] + jnp.dot(x_lo, rhi_ref[...], preferred_element_type=F32)) + t[:, N_EXPERTS:]
        m = jnp.max(logits, axis=-1, keepdims=True)
        e = jnp.exp(logits - m)
        aff_ref[rows, :] = e / jnp.sum(e, axis=-1, keepdims=True)


def _outproj_ln_router(acts, x, w, g, b, w_r):
    n = x.shape[0]
    tm = _row_tile(n, 512)
    r_hi = w_r.astype(BF16)
    r_lo = (w_r - r_hi.astype(F32)).astype(BF16)
    row = lambda width: pl.BlockSpec((tm, width), lambda i: (i, 0))
    const = lambda shape: pl.BlockSpec(shape, lambda i: (0, 0))
    return pl.pallas_call(
        functools.partial(_outproj_kernel, len(acts)),
        grid=(n // tm,),
        in_specs=[row(a.shape[1]) for a in acts] + [
            row(D_MODEL), const((D_MODEL, D_MODEL)), const((1, D_MODEL)), const((1, D_MODEL)),
            const((D_MODEL, 2 * N_EXPERTS)), const((D_MODEL, N_EXPERTS)),
        ],
        out_specs=[row(D_MODEL), row(PACKED), row(N_EXPERTS)],
        out_shape=[
            jax.ShapeDtypeStruct((n, D_MODEL), F32),
            jax.ShapeDtypeStruct((n, PACKED), U32),
            jax.ShapeDtypeStruct((n, N_EXPERTS), F32),
        ],
        compiler_params=_cparams(("parallel",)),
        name="outproj_ln_router",
    )(*acts, x, w, g.reshape(1, -1), b.reshape(1, -1), jnp.concatenate([r_hi, r_lo], axis=1), r_hi)


def _topk_kernel(cap, aff_ref, gate_ref, mask_ref):
    n = aff_ref.shape[1]
    bits = pltpu.bitcast(aff_ref[...], jnp.int32)
    count = lambda m: jnp.sum(m.astype(jnp.int32), axis=1, keepdims=True)

    def value_step(i, t):
        cand = t | (jnp.int32(1) << (30 - i))
        return jnp.where(count(bits >= cand) >= cap, cand, t)

    thr = lax.fori_loop(0, 31, value_step, jnp.zeros((N_EXPERTS, 1), jnp.int32))
    gt = bits > thr
    eq = bits == thr
    need = cap - count(gt)
    idx = lax.broadcasted_iota(jnp.int32, (N_EXPERTS, n), 1)
    nbits = max(int(n).bit_length(), 1)

    def index_step(i, j0):
        cand = j0 | (jnp.int32(1) << (nbits - 1 - i))
        return jnp.where(count(eq & (idx < cand)) < need, cand, j0)

    j0 = lax.fori_loop(0, nbits, index_step, jnp.zeros((N_EXPERTS, 1), jnp.int32))
    sel = gt | (eq & (idx <= j0))
    gate_ref[...] = jnp.where(sel, aff_ref[...], 0.0)
    mask_ref[...] = sel.astype(jnp.int32)


def _expert_choice(aff_t, cap):
    n = aff_t.shape[1]
    return pl.pallas_call(
        functools.partial(_topk_kernel, cap),
        out_shape=[
            jax.ShapeDtypeStruct((N_EXPERTS, n), F32),
            jax.ShapeDtypeStruct((N_EXPERTS, n), jnp.int32),
        ],
        compiler_params=pltpu.CompilerParams(vmem_limit_bytes=VMEM_LIMIT),
        name="expert_choice_topk",
    )(aff_t)


def _ffn_kernel(x_ref, g_ref, wg_ref, wu_ref, wd_ref, o_ref):
    x = _unpack_rows(x_ref[...])
    hg = jnp.dot(x, wg_ref[...], preferred_element_type=F32)
    hu = jnp.dot(x, wu_ref[...], preferred_element_type=F32)
    h = (hg * jax.nn.sigmoid(hg) * hu).astype(BF16)
    y = jnp.dot(h, wd_ref[...], preferred_element_type=F32)
    o_ref[...] = _pack_rows(y * g_ref[:, 0:1])


def _expert_ffn(xe, gate, wg, wu, wd, layer, cap):
    rows = xe.shape[0]
    tm = _row_tile(cap, 1024)
    nr = cap // tm
    wspec = pl.BlockSpec((None, None, D_MODEL, D_MODEL), lambda e, r: (layer, e, 0, 0))
    return pl.pallas_call(
        _ffn_kernel,
        grid=(N_EXPERTS, nr),
        in_specs=[
            pl.BlockSpec((tm, PACKED), lambda e, r: (e * nr + r, 0)),
            pl.BlockSpec((tm, LANES), lambda e, r: (e * nr + r, 0)),
            wspec, wspec, wspec,
        ],
        out_specs=pl.BlockSpec((tm, PACKED), lambda e, r: (e * nr + r, 0)),
        out_shape=jax.ShapeDtypeStruct((rows, PACKED), U32),
        compiler_params=_cparams(("parallel", "arbitrary")),
        name="expert_ffn",
    )(xe, gate, wg, wu, wd)


def _compact_kernel(J, mask_ref, gate_ref, dest_ref, idx_ref, gsel_ref, dslot_ref):
    NC = mask_ref.shape[0]
    cap = idx_ref.shape[1]
    nt = lambda a, b: lax.dot_general(a, b, (((1,), (1,)), ((), ())), preferred_element_type=F32)
    iota = lambda shape, d: lax.broadcasted_iota(jnp.int32, shape, d)
    m = mask_ref[...].astype(F32).astype(BF16)
    tril = (iota((LANES, LANES), 1) <= iota((LANES, LANES), 0)).astype(BF16)
    eye = (iota((LANES, LANES), 1) == iota((LANES, LANES), 0)).astype(BF16)
    lt = nt(tril, m).astype(BF16)
    tot_b = jnp.dot(m, jnp.ones((LANES, LANES), BF16), preferred_element_type=F32)
    trilc = (iota((NC, NC), 1) <= iota((NC, NC), 0)).astype(BF16)
    cum_col = jnp.dot(trilc, tot_b.astype(BF16), preferred_element_type=F32)[:, 0:1]
    tot_col = tot_b[:, 0:1]
    g = gate_ref[...]
    g0 = g.astype(BF16)
    r1 = g - g0.astype(F32)
    g1 = r1.astype(BF16)
    g2 = (r1 - g1.astype(F32)).astype(BF16)
    gts = [nt(eye, p).astype(BF16) for p in (g0, g1, g2)]
    d = dest_ref[...]
    dts = [nt(eye, p.astype(F32).astype(BF16)).astype(BF16)
           for p in (d & 255, (d >> 8) & 255, d >> 16)]
    chunk_id = iota((NC, J), 0).astype(F32)
    tok_id = iota((LANES, J), 0).astype(F32)
    colsum = lambda a: jnp.sum(a, axis=0, keepdims=True)

    def block(jb, carry):
        j0 = pl.multiple_of(jb * J, J)
        slot = (j0 + iota((1, J), 1)).astype(F32)
        before = cum_col <= slot
        c_j = colsum(before.astype(F32))
        rank = slot - colsum(jnp.where(before, tot_col, 0.0))
        onehot = (chunk_id == c_j).astype(BF16)
        within = colsum((jnp.dot(lt, onehot, preferred_element_type=F32) <= rank).astype(F32))
        tsel = tok_id == within
        pick = lambda a: colsum(jnp.where(tsel, jnp.dot(a, onehot, preferred_element_type=F32), 0.0))
        idx_ref[:, pl.ds(j0, J)] = (c_j * LANES + within).astype(jnp.int32)
        gsel_ref[:, pl.ds(j0, J)] = (pick(gts[0]) + pick(gts[1])) + pick(gts[2])
        dslot_ref[:, pl.ds(j0, J)] = (pick(dts[0]) + 256.0 * pick(dts[1]) + 65536.0 * pick(dts[2])).astype(jnp.int32)
        return carry

    lax.fori_loop(0, cap // J, block, 0)


def _compact(mask_t, gate_t, dest_t, cap):
    n = mask_t.shape[1]
    nc = n // LANES
    J = _row_tile(cap, COMPACT_BLOCK)
    chunked = lambda a: a.reshape(N_EXPERTS, nc, LANES)
    ispec = pl.BlockSpec((None, nc, LANES), lambda e: (e, 0, 0))
    ospec = pl.BlockSpec((None, 1, cap), lambda e: (e, 0, 0))
    oshape = lambda dt: jax.ShapeDtypeStruct((N_EXPERTS, 1, cap), dt)
    idx, gsel, dslot = pl.pallas_call(
        functools.partial(_compact_kernel, J),
        grid=(N_EXPERTS,),
        in_specs=[ispec, ispec, ispec],
        out_specs=[ospec, ospec, ospec],
        out_shape=[oshape(jnp.int32), oshape(F32), oshape(jnp.int32)],
        compiler_params=_cparams(("parallel",)),
        name="slot_lists",
    )(chunked(mask_t), chunked(gate_t), chunked(dest_t))
    return idx.reshape(-1), gsel.reshape(-1), dslot.reshape(-1)


def _sc_rows(data, idx, gather):
    R = idx.shape[0]
    D = data.shape[1]
    info = plsc.get_sparse_core_info()
    nw = info.num_cores * info.num_subcores
    per_w = R // nw
    assert per_w * nw == R and per_w % SC_WINDOW == 0
    mesh = plsc.VectorSubcoreMesh(core_axis_name="c", subcore_axis_name="s")

    @functools.partial(
        pl.kernel, mesh=mesh,
        out_type=jax.ShapeDtypeStruct((R, D), data.dtype),
        scratch_types=[pltpu.VMEM((SC_WINDOW,), jnp.int32),
                       pltpu.VMEM((SC_WINDOW, D), data.dtype),
                       pltpu.SemaphoreType.DMA],
        name="sc_gather_rows" if gather else "sc_scatter_rows",
    )
    def move(data_hbm, idx_hbm, out_hbm, idx_v, rows_v, sem):
        wid = lax.axis_index("s") * info.num_cores + lax.axis_index("c")

        @pl.loop(0, per_w // SC_WINDOW)
        def _(j):
            base = wid * per_w + j * SC_WINDOW
            pltpu.sync_copy(idx_hbm.at[pl.ds(base, SC_WINDOW)], idx_v)
            if gather:
                pltpu.async_copy(data_hbm.at[idx_v], rows_v, sem).wait()
                pltpu.sync_copy(rows_v, out_hbm.at[pl.ds(base, SC_WINDOW)])
            else:
                pltpu.sync_copy(data_hbm.at[pl.ds(base, SC_WINDOW)], rows_v)
                pltpu.async_copy(rows_v, out_hbm.at[idx_v], sem).wait()

    return move(data, idx)


def _gather_rows(table, idx):
    return _sc_rows(table, idx, True)


def _scatter_rows(src, idx):
    return _sc_rows(src, idx, False)


def _combine_kernel(n_rows, n_tiles, base_ref, x_ref, off_ref, end_ref, g_ref, b_ref, z_hbm, o_ref,
                    zbuf, sem, acc_ref):
    i = pl.program_id(0)

    def span(t):
        b0 = (base_ref[t] // ROW_ALIGN) * ROW_ALIGN
        return b0, (base_ref[t + 1] - b0 + Z_CHUNK - 1) // Z_CHUNK

    def row0(b0, k):
        return pl.multiple_of(jnp.minimum(b0 + k * Z_CHUNK, n_rows - Z_CHUNK), ROW_ALIGN)

    def copy(t, b0, k):
        slot = (t % 2) * Z_SLOTS + k % Z_SLOTS
        return pltpu.make_async_copy(z_hbm.at[pl.ds(row0(b0, k), Z_CHUNK)], zbuf.at[slot], sem.at[slot])

    def prime(t):
        b0, cnt = span(t)
        for s in range(Z_SLOTS):
            @pl.when(s < cnt)
            def _():
                copy(t, b0, s).start()

    @pl.when(i == 0)
    def _():
        prime(0)

    @pl.when(i + 1 < n_tiles)
    def _():
        prime(i + 1)

    base, nck = span(i)
    acc_ref[...] = jnp.zeros_like(acc_ref)
    off_col = off_ref[:, 0:1]
    end_col = end_ref[:, 0:1]
    lane = lax.broadcasted_iota(jnp.int32, (1, Z_CHUNK), 1)

    def body(k, carry):
        copy(i, base, k).wait()

        @pl.when((k >= 1) & (k - 1 + Z_SLOTS < nck))
        def _():
            copy(i, base, k - 1 + Z_SLOTS).start()

        d = row0(base, k) + lane
        own = (d >= off_col) & (d < end_col) & (d >= base + k * Z_CHUNK)
        z = _unpack_rows(zbuf[(i % 2) * Z_SLOTS + k % Z_SLOTS])
        acc_ref[...] += jnp.dot(own.astype(BF16), z, preferred_element_type=F32)
        return carry

    lax.fori_loop(0, nck, body, 0)

    y = ALPHA * x_ref[...] + acc_ref[...]
    mu = jnp.mean(y, axis=-1, keepdims=True)
    yc = y - mu
    var = jnp.mean(yc * yc, axis=-1, keepdims=True)
    o_ref[...] = yc * lax.rsqrt(var + LN_EPS) * g_ref[...] + b_ref[...]


def _combine_ln(x1, z, off, cnt, g, b):
    n = x1.shape[0]
    n_rows = z.shape[0]
    T = _row_tile(n, TOKEN_TILE)
    assert n_rows >= Z_CHUNK and n_rows % ROW_ALIGN == 0
    tile_base = jnp.concatenate([off[::T], jnp.full((1,), n_rows, jnp.int32)])
    offb = jnp.broadcast_to(off[:, None], (n, LANES))
    endb = jnp.broadcast_to((off + cnt)[:, None], (n, LANES))
    row = lambda w: pl.BlockSpec((T, w), lambda i, tb: (i, 0))
    vec = pl.BlockSpec((1, D_MODEL), lambda i, tb: (0, 0))
    return pl.pallas_call(
        functools.partial(_combine_kernel, n_rows, n // T),
        grid_spec=pltpu.PrefetchScalarGridSpec(
            num_scalar_prefetch=1,
            grid=(n // T,),
            in_specs=[row(D_MODEL), row(LANES), row(LANES), vec, vec, pl.BlockSpec(memory_space=pl.ANY)],
            out_specs=row(D_MODEL),
            scratch_shapes=[pltpu.VMEM((2 * Z_SLOTS, Z_CHUNK, PACKED), U32),
                            pltpu.SemaphoreType.DMA((2 * Z_SLOTS,)),
                            pltpu.VMEM((T, D_MODEL), F32)],
        ),
        out_shape=jax.ShapeDtypeStruct((n, D_MODEL), F32),
        compiler_params=_cparams(("arbitrary",)),
        name="combine_ln",
    )(tile_base, x1, offb, endb, g.reshape(1, -1), b.reshape(1, -1), z)


def _inproj1_kernel(x_ref, w_ref, tab_ref, gq_ref, gk_ref, seg_ref, q_ref, k_ref, v_ref):
    tm = x_ref.shape[0]
    sub = min(tm, MM_ROWS)
    sh = HEAD_DIM // 4
    seg = seg_ref[...]
    lane = lax.broadcasted_iota(jnp.int32, (1, LANES), 1)
    ones_col = jnp.where(lane == HEAD_DIM, 1.0, 0.0).astype(F32)
    for r in range(tm // sub):
        rows = slice(r * sub, (r + 1) * sub)
        h = jnp.dot(x_ref[rows, :].astype(BF16), w_ref[...], preferred_element_type=F32)

        def norm_rope(xc, gain, t0):
            ms = jnp.dot((xc * xc).astype(BF16), seg, preferred_element_type=F32)
            xn = xc * lax.rsqrt(ms + RMS_EPS) * gain
            return _rope_chunk(xn, tab_ref[t0, rows, :], tab_ref[t0 + 1, rows, :], tab_ref[t0 + 2, rows, :], sh)

        for c in range(C_Q // LANES):
            xc = h[:, c * LANES:(c + 1) * LANES]
            q_ref[rows, c * LANES:(c + 1) * LANES] = norm_rope(xc, gq_ref[...], 0).astype(BF16)
        for c in range(C_KV // LANES):
            xc = h[:, C_Q + c * LANES:C_Q + (c + 1) * LANES]
            kc = norm_rope(xc, gk_ref[...], 3).astype(BF16)
            k_ref[2 * c, rows, :] = kc[:, :HEAD_DIM]
            k_ref[2 * c + 1, rows, :] = kc[:, HEAD_DIM:]
            vc = h[:, C_Q + C_KV + c * LANES:C_Q + C_KV + (c + 1) * LANES]
            v_ref[2 * c, rows, :] = jnp.where(lane < HEAD_DIM, vc, ones_col).astype(BF16)
            v_ref[2 * c + 1, rows, :] = jnp.where(lane < HEAD_DIM, pltpu.roll(vc, HEAD_DIM, axis=1),
                                                  ones_col).astype(BF16)


def _inproj1(x, w, tab, gq, gk, S):
    n = x.shape[0]
    tm = _row_tile(S, 512)
    nt = S // tm
    lane_head = np.arange(LANES) // HEAD_DIM
    seg = jnp.asarray((lane_head[:, None] == lane_head[None, :]) / HEAD_DIM, BF16)
    tile2 = lambda g: jnp.concatenate([g, g]).reshape(1, LANES)
    const = lambda shape: pl.BlockSpec(shape, lambda i: (0,) * len(shape))
    return pl.pallas_call(
        _inproj1_kernel,
        grid=(n // tm,),
        in_specs=[
            pl.BlockSpec((tm, D_MODEL), lambda i: (i, 0)),
            const((D_MODEL, C_IN)),
            pl.BlockSpec((6, tm, LANES), lambda i: (0, i % nt, 0)),
            const((1, LANES)), const((1, LANES)), const((LANES, LANES)),
        ],
        out_specs=[
            pl.BlockSpec((tm, C_Q), lambda i: (i, 0)),
            pl.BlockSpec((AX_KV_HEADS, tm, HEAD_DIM), lambda i: (0, i, 0)),
            pl.BlockSpec((AX_KV_HEADS, tm, LANES), lambda i: (0, i, 0)),
        ],
        out_shape=[
            jax.ShapeDtypeStruct((n, C_Q), BF16),
            jax.ShapeDtypeStruct((AX_KV_HEADS, n, HEAD_DIM), BF16),
            jax.ShapeDtypeStruct((AX_KV_HEADS, n, LANES), BF16),
        ],
        compiler_params=_cparams(("parallel",)),
        name="inproj1",
    )(x, w, tab, tile2(gq), tile2(gk), seg)


def _axattn_kernel(q_ref, k_ref, v_ref, o_ref):
    G = AX_HEADS // AX_KV_HEADS
    TQ = q_ref.shape[0]
    chains = [(slice(r * AX_CHAIN_ROWS, (r + 1) * AX_CHAIN_ROWS), slice(j * HEAD_DIM, (j + 1) * HEAD_DIM))
              for r in range(TQ // AX_CHAIN_ROWS) for j in range(G)]

    def scores(c):
        rows, cols = chains[c]
        return lax.dot_general(q_ref[rows, cols], k_ref[...], (((1,), (1,)), ((), ())),
                               preferred_element_type=F32)

    s_next = scores(0)
    for c, (rows, cols) in enumerate(chains):
        s = s_next
        if c + 1 < len(chains):
            s_next = scores(c + 1)
        m = jnp.max(s, axis=-1, keepdims=True)
        p = jnp.exp2(s - m)
        o = jnp.dot(p.astype(BF16), v_ref[...], preferred_element_type=F32)
        o_ref[rows, cols] = (o[:, :HEAD_DIM] / o[:, HEAD_DIM:HEAD_DIM + 1]).astype(BF16)


def _axial_attention(q, k, v, B, S):
    n = q.shape[0]
    TQ = _row_tile(S, AX_Q_TILE)
    nq = S // TQ
    G = AX_HEADS // AX_KV_HEADS
    qspec = pl.BlockSpec((TQ, G * HEAD_DIM), lambda b, g, i: (b * nq + i, g))
    kspec = pl.BlockSpec((None, None, S, HEAD_DIM), lambda b, g, i: (g, b, 0, 0))
    vspec = pl.BlockSpec((None, None, S, LANES), lambda b, g, i: (g, b, 0, 0))
    return pl.pallas_call(
        _axattn_kernel,
        grid=(B, AX_KV_HEADS, nq),
        in_specs=[qspec, kspec, vspec],
        out_specs=qspec,
        out_shape=jax.ShapeDtypeStruct((n, C_Q), BF16),
        compiler_params=_cparams(("parallel", "parallel", "arbitrary")),
        name="axial_attention",
    )(q, k, v)


def _moe(x1, xb, aff, wg, wu, wd, layer, ln_g, ln_b, n):
    cap = CAPACITY_FACTOR * n // N_EXPERTS
    gate_t, mask_t = _expert_choice(aff.T, cap)
    cnt = jnp.sum(mask_t, axis=0)
    off = jnp.cumsum(cnt) - cnt
    dest_t = off[None, :] + jnp.cumsum(mask_t, axis=0) - mask_t
    idx, gsel, dslot = _compact(mask_t, gate_t, dest_t, cap)
    xe = _gather_rows(xb, idx)
    gb = jnp.broadcast_to(gsel[:, None], (gsel.shape[0], LANES))
    ye = _expert_ffn(xe, gb, wg, wu, wd, layer, cap)
    z = _scatter_rows(ye, dslot)
    return _combine_ln(x1, z, off, cnt, ln_g, ln_b)


def _trunk(x, p):
    B, S, _ = x.shape
    n = B * S
    xf = x.reshape(n, D_MODEL)
    tab_w = _tables_window(S)
    tab_a = _tables_axial(S)
    for layer in range(DEPTH):
        i = layer // 2
        if layer % 2 == 0:
            ug, q, k, v = _inproj0(xf, p['ab_w_in'][i], tab_w, S)
            conv = _conv_branch(ug, p['ab_conv_w'][i], p['ab_conv_b'][i], p['ab_conv_ln_g'][i],
                                p['ab_conv_ln_b'][i], S)
            att = _window_attention(q, k, v, p['ab_sink'][i], B, S)
            acts, w_out = [conv, att], p['ab_w_out'][i]
        else:
            q, k, v = _inproj1(xf, p['c_w_in'][i], tab_a, p['c_q_norm'][i], p['c_k_norm'][i], S)
            k = k.reshape(AX_KV_HEADS, B, S, HEAD_DIM)
            v = v.reshape(AX_KV_HEADS, B, S, LANES)
            acts, w_out = [_axial_attention(q, k, v, B, S)], p['c_w_out'][i]
        x1, xb, aff = _outproj_ln_router(acts, xf, w_out, p['ln1_g'][layer], p['ln1_b'][layer],
                                         p['moe_router'][layer])
        xf = _moe(x1, xb, aff, p['moe_w_gate'], p['moe_w_up'], p['moe_w_down'], layer,
                  p['ln2_g'][layer], p['ln2_b'][layer], n)
    return xf.reshape(B, S, D_MODEL)


def kernel(x_prompt, x_sample, ab_w_in, ab_conv_w, ab_conv_b, ab_conv_ln_g, ab_conv_ln_b, ab_sink, ab_w_out, c_w_in, c_q_norm, c_k_norm, c_w_out, ln1_g, ln1_b, moe_router, moe_w_gate, moe_w_up, moe_w_down, ln2_g, ln2_b):
    p = dict(
        ab_w_in=ab_w_in.astype(BF16), ab_conv_w=ab_conv_w, ab_conv_b=ab_conv_b,
        ab_conv_ln_g=ab_conv_ln_g, ab_conv_ln_b=ab_conv_ln_b, ab_sink=ab_sink,
        ab_w_out=ab_w_out.astype(BF16), c_w_in=c_w_in.astype(BF16), c_q_norm=c_q_norm,
        c_k_norm=c_k_norm, c_w_out=c_w_out.astype(BF16), ln1_g=ln1_g, ln1_b=ln1_b,
        moe_router=moe_router, moe_w_gate=moe_w_gate.astype(BF16), moe_w_up=moe_w_up.astype(BF16),
        moe_w_down=moe_w_down.astype(BF16), ln2_g=ln2_g, ln2_b=ln2_b,
    )
    return (_trunk(x_prompt, p), _trunk(x_sample, p))
```

```python
import functools

import jax
import jax.numpy as jnp
import numpy as np
from jax import lax
from jax.experimental import pallas as pl
from jax.experimental.pallas import tpu as pltpu
from jax.experimental.pallas import tpu_sc as plsc

D_MODEL = 1024
DEPTH = 2
GRID_W = 64
BLOCK = 128
HEAD_DIM = 64
CONV_DIM = 512
CONV_WIDTH = 31
WIN_HEADS = 8
WIN_KV_HEADS = 2
WINDOW = 128
ROT_DIM_B = 16
ROPE_THETA_B = 500000.0
AX_HEADS = 16
AX_KV_HEADS = 4
ROPE_THETA_C = 10000.0
N_EXPERTS = 16
CAPACITY_FACTOR = 2
ALPHA = (2 * DEPTH) ** 0.25
AB_Q = WIN_HEADS * HEAD_DIM
AB_KV = WIN_KV_HEADS * HEAD_DIM
AB_IN = 2 * CONV_DIM + AB_Q + 2 * AB_KV
C_Q = AX_HEADS * HEAD_DIM
C_KV = AX_KV_HEADS * HEAD_DIM
C_IN = C_Q + 2 * C_KV
LN_EPS = 1e-5
LOG2E = 1.4426950408889634
RMS_EPS = 1e-6

LANES = 128
SUBLANES = 8
CONV_ROWS = 64
CONV_SPAN = 24
MM_ROWS = 128
CONV_HALO = 16
VMEM_LIMIT = 48 * 1024 * 1024

TOKEN_TILE = 256
Z_CHUNK = 256
Z_SLOTS = 4
ROW_ALIGN = 16
SC_WINDOW = 64
COMPACT_BLOCK = 256
COMPACT_UNROLL = 4
AX_Q_TILE = 1024
AX_CHAIN_ROWS = 512

BF16 = jnp.bfloat16
F32 = jnp.float32
U32 = jnp.uint32
PACKED = D_MODEL // 2


def _pack_rows(x):
    b = lax.bitcast_convert_type(x.astype(BF16).astype(F32), U32)
    return (b[:, :PACKED] >> 16) | (b[:, PACKED:] & jnp.uint32(0xFFFF0000))


def _unpack_rows(w):
    lo = lax.bitcast_convert_type(w << 16, F32)
    hi = lax.bitcast_convert_type(w & jnp.uint32(0xFFFF0000), F32)
    return jnp.concatenate([lo, hi], axis=1).astype(BF16)


def _cparams(sem):
    return pltpu.CompilerParams(dimension_semantics=sem, vmem_limit_bytes=VMEM_LIMIT)


def _row_tile(n, pref):
    t = min(n, pref)
    assert n % t == 0
    return t


def _tables(S, groups, q_scale):
    lane = np.arange(LANES) % HEAD_DIM
    ang = jnp.zeros((S, LANES), F32)
    first = np.zeros(LANES, bool)
    second = np.zeros(LANES, bool)
    for lo, shift, theta, pos in groups:
        dim = 2 * shift
        inv = theta ** (-jnp.arange(0, dim, 2, dtype=F32) / dim)
        member = (lane >= lo) & (lane < lo + dim)
        inv_lane = jnp.where(member, inv[np.where(member, (lane - lo) % shift, 0)], 0.0)
        ang = ang + pos[:, None] * inv_lane[None, :]
        first |= member & (lane - lo < shift)
        second |= member & (lane - lo >= shift)
    cos, sin = jnp.cos(ang), jnp.sin(ang)
    sa = jnp.where(first, -sin, 0.0)
    sb = jnp.where(second, sin, 0.0)
    return jnp.stack([cos * q_scale, sa * q_scale, sb * q_scale, cos, sa, sb])


def _tables_window(S):
    pos = jnp.arange(S, dtype=F32)
    return _tables(S, [(0, ROT_DIM_B // 2, ROPE_THETA_B, pos)], HEAD_DIM ** -0.5)


def _tables_axial(S):
    rows = S // GRID_W
    gr, gc = jnp.meshgrid(jnp.arange(rows), jnp.arange(GRID_W), indexing='ij')
    groups = [(0, HEAD_DIM // 4, ROPE_THETA_C, gr.reshape(-1).astype(F32)),
              (HEAD_DIM // 2, HEAD_DIM // 4, ROPE_THETA_C, gc.reshape(-1).astype(F32))]
    return _tables(S, groups, HEAD_DIM ** -0.5 * LOG2E)


def _rope_chunk(x, c, sa, sb, shift):
    fwd = pltpu.roll(x, LANES - shift, axis=1)
    bwd = pltpu.roll(x, shift, axis=1)
    return x * c + fwd * sa + bwd * sb


def _inproj0_kernel(x_ref, w_ref, tab_ref, ug_ref, q_ref, k_ref, v_ref):
    tm = x_ref.shape[0]
    sub = min(tm, MM_ROWS)
    sh = ROT_DIM_B // 2
    q0 = 2 * CONV_DIM
    k0 = q0 + AB_Q
    for r in range(tm // sub):
        rows = slice(r * sub, (r + 1) * sub)
        tab = lambda t: tab_ref[t, rows, :]
        h = jnp.dot(x_ref[rows, :].astype(BF16), w_ref[...], preferred_element_type=F32)
        ug_ref[rows, :] = h[:, :q0].astype(BF16)
        for c in range(AB_Q // LANES):
            xc = h[:, q0 + c * LANES:q0 + (c + 1) * LANES]
            q_ref[rows, c * LANES:(c + 1) * LANES] = _rope_chunk(xc, tab(0), tab(1), tab(2), sh).astype(BF16)
        for c in range(AB_KV // LANES):
            xc = h[:, k0 + c * LANES:k0 + (c + 1) * LANES]
            k_ref[rows, c * LANES:(c + 1) * LANES] = _rope_chunk(xc, tab(3), tab(4), tab(5), sh).astype(BF16)
        v_ref[rows, :] = h[:, k0 + AB_KV:].astype(BF16)


def _inproj0(x, w, tab, S):
    n = x.shape[0]
    tm = _row_tile(S, 512)
    nt = S // tm
    return pl.pallas_call(
        _inproj0_kernel,
        grid=(n // tm,),
        in_specs=[
            pl.BlockSpec((tm, D_MODEL), lambda i: (i, 0)),
            pl.BlockSpec((D_MODEL, AB_IN), lambda i: (0, 0)),
            pl.BlockSpec((6, tm, LANES), lambda i: (0, i % nt, 0)),
        ],
        out_specs=[
            pl.BlockSpec((tm, 2 * CONV_DIM), lambda i: (i, 0)),
            pl.BlockSpec((tm, AB_Q), lambda i: (i, 0)),
            pl.BlockSpec((tm, AB_KV), lambda i: (i, 0)),
            pl.BlockSpec((tm, AB_KV), lambda i: (i, 0)),
        ],
        out_shape=[
            jax.ShapeDtypeStruct((n, 2 * CONV_DIM), BF16),
            jax.ShapeDtypeStruct((n, AB_Q), BF16),
            jax.ShapeDtypeStruct((n, AB_KV), BF16),
            jax.ShapeDtypeStruct((n, AB_KV), BF16),
        ],
        compiler_params=_cparams(("parallel",)),
        name="inproj0",
    )(x, w, tab)


def _conv_kernel(nt, cur_ref, prev_ref, next_ref, w_ref, b_ref, g_ref, beta_ref, o_ref, buf_ref, shift_ref, hc_ref):
    T = cur_ref.shape[0]
    si = pl.program_id(0) % nt

    def glu(blk):
        u = blk[:, :CONV_DIM].astype(F32)
        gt = blk[:, CONV_DIM:].astype(F32)
        return u * jax.nn.sigmoid(gt)

    pv = jnp.where(si > 0, 1.0, 0.0).astype(F32)
    nv = jnp.where(si < nt - 1, 1.0, 0.0).astype(F32)
    buf_ref[0:CONV_HALO, :] = glu(prev_ref[...]) * pv
    buf_ref[CONV_HALO:CONV_HALO + T, :] = glu(cur_ref[...])
    buf_ref[CONV_HALO + T:, :] = glu(next_ref[...]) * nv
    L = shift_ref.shape[1]
    for r in range(1, SUBLANES):
        shift_ref[r - 1] = buf_ref[r:r + L, :]

    off = CONV_HALO - CONV_WIDTH // 2

    def rows(rb, carry):
        r0 = pl.multiple_of(rb * CONV_ROWS, CONV_ROWS)
        for c in range(CONV_DIM // LANES):
            cols = slice(c * LANES, (c + 1) * LANES)
            acc = jnp.zeros((CONV_ROWS, LANES), F32)
            for r in range(SUBLANES):
                src = buf_ref if r == 0 else shift_ref.at[r - 1]
                xr = src[pl.ds(r0, CONV_ROWS + CONV_SPAN), cols]
                for a in range(CONV_SPAN // SUBLANES + 1):
                    j = SUBLANES * a + r - off
                    if 0 <= j < CONV_WIDTH:
                        wj = w_ref[j * SUBLANES:(j + 1) * SUBLANES, cols]
                        wt = jnp.concatenate([wj] * (CONV_ROWS // SUBLANES), axis=0)
                        acc = acc + xr[SUBLANES * a:SUBLANES * a + CONV_ROWS] * wt
            hc_ref[pl.ds(r0, CONV_ROWS), cols] = acc + b_ref[:, cols]
        return carry

    lax.fori_loop(0, T // CONV_ROWS, rows, 0)
    hcv = hc_ref[...]
    mu = jnp.mean(hcv, axis=-1, keepdims=True)
    xc = hcv - mu
    var = jnp.mean(xc * xc, axis=-1, keepdims=True)
    y = xc * lax.rsqrt(var + LN_EPS) * g_ref[...] + beta_ref[...]
    o_ref[...] = (y * jax.nn.sigmoid(y)).astype(BF16)


def _conv_branch(ug, w, b, g, beta, S):
    n = ug.shape[0]
    T = _row_tile(S, 512)
    nt = S // T
    r = T // CONV_HALO
    nh = n // CONV_HALO
    vec = lambda: pl.BlockSpec((1, CONV_DIM), lambda i: (0, 0))
    return pl.pallas_call(
        functools.partial(_conv_kernel, nt),
        grid=(n // T,),
        in_specs=[
            pl.BlockSpec((T, 2 * CONV_DIM), lambda i: (i, 0)),
            pl.BlockSpec((CONV_HALO, 2 * CONV_DIM), lambda i: (jnp.maximum(i * r - 1, 0), 0)),
            pl.BlockSpec((CONV_HALO, 2 * CONV_DIM), lambda i: (jnp.minimum((i + 1) * r, nh - 1), 0)),
            pl.BlockSpec((CONV_WIDTH * SUBLANES, CONV_DIM), lambda i: (0, 0)),
            vec(), vec(), vec(),
        ],
        out_specs=pl.BlockSpec((T, CONV_DIM), lambda i: (i, 0)),
        out_shape=jax.ShapeDtypeStruct((n, CONV_DIM), BF16),
        scratch_shapes=[pltpu.VMEM((T + 2 * CONV_HALO, CONV_DIM), F32),
                        pltpu.VMEM((SUBLANES - 1, T + 2 * CONV_HALO - SUBLANES, CONV_DIM), F32),
                        pltpu.VMEM((T, CONV_DIM), F32)],
        compiler_params=_cparams(("parallel",)),
        name="conv_branch",
    )(ug, ug, ug, jnp.repeat(w, SUBLANES, axis=0), b.reshape(1, -1), g.reshape(1, -1), beta.reshape(1, -1))


def _winattn_kernel(nq, sink_ref, q_ref, kp_ref, kc_ref, kn_ref, vp_ref, vc_ref, vn_ref, o_ref):
    qi = pl.program_id(1)
    G = WIN_HEADS // WIN_KV_HEADS
    M = G * BLOCK
    row = lax.broadcasted_iota(jnp.int32, (M, 3 * BLOCK), 0) % BLOCK
    col = lax.broadcasted_iota(jnp.int32, (M, 3 * BLOCK), 1)
    band = jnp.abs(col - BLOCK - row) <= WINDOW
    head = lax.broadcasted_iota(jnp.int32, (M, 1), 0) // BLOCK
    chains = [(half, g) for half in range(2) for g in range(WIN_KV_HEADS)]

    def operands(half, g):
        sl = slice(g * HEAD_DIM, (g + 1) * HEAD_DIM)
        krefs, vrefs = ((kp_ref, kc_ref), (vp_ref, vc_ref)) if half == 0 else ((kc_ref, kn_ref), (vc_ref, vn_ref))
        return (jnp.concatenate([r[:, sl] for r in krefs], axis=0),
                jnp.concatenate([r[:, sl] for r in vrefs], axis=0))

    def scores(half, g):
        rows = slice(half * BLOCK, (half + 1) * BLOCK)
        qs = jnp.concatenate([q_ref[rows, h * HEAD_DIM:(h + 1) * HEAD_DIM] for h in range(g * G, (g + 1) * G)],
                             axis=0)
        return lax.dot_general(qs, operands(half, g)[0], (((1,), (1,)), ((), ())), preferred_element_type=F32)

    s_next = scores(*chains[0])
    for c, (half, g) in enumerate(chains):
        rows = slice(half * BLOCK, (half + 1) * BLOCK)
        s = s_next
        if c + 1 < len(chains):
            s_next = scores(*chains[c + 1])
        if half == 0:
            valid = band & ((qi > 0) | (col >= BLOCK))
        else:
            valid = band & ((qi < nq - 1) | (col < 2 * BLOCK))
        heads = range(g * G, (g + 1) * G)
        s = jnp.where(valid, s, -jnp.inf)
        sk = jnp.zeros((M, 1), F32)
        for j, h in enumerate(heads):
            sk = jnp.where(head == j, sink_ref[h], sk)
        m = jnp.maximum(jnp.max(s, axis=-1, keepdims=True), sk)
        p = jnp.exp(s - m)
        den = jnp.sum(p, axis=-1, keepdims=True) + jnp.exp(sk - m)
        o = jnp.dot(p.astype(BF16), operands(half, g)[1], preferred_element_type=F32) / den
        for j, h in enumerate(heads):
            o_ref[rows, h * HEAD_DIM:(h + 1) * HEAD_DIM] = o[j * BLOCK:(j + 1) * BLOCK].astype(BF16)


def _window_attention(q, k, v, sink, B, S):
    n = q.shape[0]
    nb = S // BLOCK
    nq = nb // 2
    cur = lambda b, i: (b * nq + i, 0)
    prev = lambda b, i: (b * nb + jnp.maximum(2 * i - 1, 0), 0)
    nxt = lambda b, i: (b * nb + jnp.minimum(2 * i + 2, nb - 1), 0)
    edge = lambda f: pl.BlockSpec((BLOCK, AB_KV), f)
    mid = pl.BlockSpec((2 * BLOCK, AB_KV), cur)
    return pl.pallas_call(
        functools.partial(_winattn_kernel, nq),
        grid=(B, nq),
        in_specs=[
            pl.BlockSpec(memory_space=pltpu.SMEM),
            pl.BlockSpec((2 * BLOCK, AB_Q), cur),
            edge(prev), mid, edge(nxt), edge(prev), mid, edge(nxt),
        ],
        out_specs=pl.BlockSpec((2 * BLOCK, AB_Q), cur),
        out_shape=jax.ShapeDtypeStruct((n, AB_Q), BF16),
        compiler_params=_cparams(("parallel", "parallel")),
        name="window_attention",
    )(sink, q, k, k, k, v, v, v)


def _outproj_kernel(n_act, *refs):
    acts = refs[:n_act]
    x_ref, w_ref, g_ref, b_ref, rcat_ref, rhi_ref, xo_ref, xb_ref, aff_ref = refs[n_act:]
    tm = x_ref.shape[0]
    sub = min(tm, MM_ROWS)
    def project(r):
        rows = slice(r * sub, (r + 1) * sub)
        a = acts[0][rows, :] if n_act == 1 else jnp.concatenate([ref[rows, :] for ref in acts], axis=-1)
        return jnp.dot(a, w_ref[...], preferred_element_type=F32)

    mix_next = project(0)
    for r in range(tm // sub):
        rows = slice(r * sub, (r + 1) * sub)
        mix = mix_next
        if r + 1 < tm // sub:
            mix_next = project(r + 1)
        y = ALPHA * x_ref[rows, :] + mix
        mu = jnp.mean(y, axis=-1, keepdims=True)
        yc = y - mu
        var = jnp.mean(yc * yc, axis=-1, keepdims=True)
        x1 = yc * lax.rsqrt(var + LN_EPS) * g_ref[...] + b_ref[...]
        xo_ref[rows, :] = x1
        xb_ref[rows, :] = _pack_rows(x1)
        x_hi = x1.astype(BF16)
        x_lo = (x1 - x_hi.astype(F32)).astype(BF16)
        t = jnp.dot(x_hi, rcat_ref[...], preferred_element_type=F32)
        logits = (t[:, :N_EXPERTS] + jnp.dot(x_lo, rhi_ref[...], preferred_element_type=F32)) + t[:, N_EXPERTS:]
        m = jnp.max(logits, axis=-1, keepdims=True)
        e = jnp.exp(logits - m)
        aff_ref[rows, :] = e / jnp.sum(e, axis=-1, keepdims=True)


def _outproj_ln_router(acts, x, w, g, b, w_r):
    n = x.shape[0]
    tm = _row_tile(n, 512)
    r_hi = w_r.astype(BF16)
    r_lo = (w_r - r_hi.astype(F32)).astype(BF16)
    row = lambda width: pl.BlockSpec((tm, width), lambda i: (i, 0))
    const = lambda shape: pl.BlockSpec(shape, lambda i: (0, 0))
    return pl.pallas_call(
        functools.partial(_outproj_kernel, len(acts)),
        grid=(n // tm,),
        in_specs=[row(a.shape[1]) for a in acts] + [
            row(D_MODEL), const((D_MODEL, D_MODEL)), const((1, D_MODEL)), const((1, D_MODEL)),
            const((D_MODEL, 2 * N_EXPERTS)), const((D_MODEL, N_EXPERTS)),
        ],
        out_specs=[row(D_MODEL), row(PACKED), row(N_EXPERTS)],
        out_shape=[
            jax.ShapeDtypeStruct((n, D_MODEL), F32),
            jax.ShapeDtypeStruct((n, PACKED), U32),
            jax.ShapeDtypeStruct((n, N_EXPERTS), F32),
        ],
        compiler_params=_cparams(("parallel",)),
        name="outproj_ln_router",
    )(*acts, x, w, g.reshape(1, -1), b.reshape(1, -1), jnp.concatenate([r_hi, r_lo], axis=1), r_hi)


def _topk_kernel(cap, aff_ref, gate_ref, mask_ref):
    n = aff_ref.shape[1]
    bits = pltpu.bitcast(aff_ref[...], jnp.int32)
    count = lambda m: jnp.sum(m.astype(jnp.int32), axis=1, keepdims=True)

    def value_step(i, t):
        cand = t | (jnp.int32(1) << (30 - i))
        return jnp.where(count(bits >= cand) >= cap, cand, t)

    thr = lax.fori_loop(0, 31, value_step, jnp.zeros((N_EXPERTS, 1), jnp.int32))
    gt = bits > thr
    eq = bits == thr
    need = cap - count(gt)
    idx = lax.broadcasted_iota(jnp.int32, (N_EXPERTS, n), 1)
    nbits = max(int(n).bit_length(), 1)

    def index_step(i, j0):
        cand = j0 | (jnp.int32(1) << (nbits - 1 - i))
        return jnp.where(count(eq & (idx < cand)) < need, cand, j0)

    j0 = lax.fori_loop(0, nbits, index_step, jnp.zeros((N_EXPERTS, 1), jnp.int32))
    sel = gt | (eq & (idx <= j0))
    gate_ref[...] = jnp.where(sel, aff_ref[...], 0.0)
    mask_ref[...] = sel.astype(jnp.int32)


def _expert_choice(aff_t, cap):
    n = aff_t.shape[1]
    return pl.pallas_call(
        functools.partial(_topk_kernel, cap),
        out_shape=[
            jax.ShapeDtypeStruct((N_EXPERTS, n), F32),
            jax.ShapeDtypeStruct((N_EXPERTS, n), jnp.int32),
        ],
        compiler_params=pltpu.CompilerParams(vmem_limit_bytes=VMEM_LIMIT),
        name="expert_choice_topk",
    )(aff_t)


def _ffn_kernel(x_ref, g_ref, wg_ref, wu_ref, wd_ref, o_ref):
    x = _unpack_rows(x_ref[...])
    hg = jnp.dot(x, wg_ref[...], preferred_element_type=F32)
    hu = jnp.dot(x, wu_ref[...], preferred_element_type=F32)
    h = (hg * jax.nn.sigmoid(hg) * hu).astype(BF16)
    y = jnp.dot(h, wd_ref[...], preferred_element_type=F32)
    o_ref[...] = _pack_rows(y * g_ref[:, 0:1])


def _expert_ffn(xe, gate, wg, wu, wd, layer, cap):
    rows = xe.shape[0]
    tm = _row_tile(cap, 1024)
    nr = cap // tm
    wspec = pl.BlockSpec((None, None, D_MODEL, D_MODEL), lambda e, r: (layer, e, 0, 0))
    return pl.pallas_call(
        _ffn_kernel,
        grid=(N_EXPERTS, nr),
        in_specs=[
            pl.BlockSpec((tm, PACKED), lambda e, r: (e * nr + r, 0)),
            pl.BlockSpec((tm, LANES), lambda e, r: (e * nr + r, 0)),
            wspec, wspec, wspec,
        ],
        out_specs=pl.BlockSpec((tm, PACKED), lambda e, r: (e * nr + r, 0)),
        out_shape=jax.ShapeDtypeStruct((rows, PACKED), U32),
        compiler_params=_cparams(("parallel", "arbitrary")),
        name="expert_ffn",
    )(xe, gate, wg, wu, wd)


def _compact_kernel(J, mask_ref, gate_ref, dest_ref, idx_ref, gsel_ref, dslot_ref):
    NC = mask_ref.shape[0]
    cap = idx_ref.shape[1]
    nt = lambda a, b: lax.dot_general(a, b, (((1,), (1,)), ((), ())), preferred_element_type=F32)
    iota = lambda shape, d: lax.broadcasted_iota(jnp.int32, shape, d)
    m = mask_ref[...].astype(F32).astype(BF16)
    tril = (iota((LANES, LANES), 1) <= iota((LANES, LANES), 0)).astype(BF16)
    eye = (iota((LANES, LANES), 1) == iota((LANES, LANES), 0)).astype(BF16)
    lt = nt(tril, m).astype(BF16)
    tot_b = jnp.dot(m, jnp.ones((LANES, LANES), BF16), preferred_element_type=F32)
    trilc = (iota((NC, NC), 1) <= iota((NC, NC), 0)).astype(BF16)
    cum_col = jnp.dot(trilc, tot_b.astype(BF16), preferred_element_type=F32)[:, 0:1]
    tot_col = tot_b[:, 0:1]
    g = gate_ref[...]
    g0 = g.astype(BF16)
    r1 = g - g0.astype(F32)
    g1 = r1.astype(BF16)
    g2 = (r1 - g1.astype(F32)).astype(BF16)
    gts = [nt(eye, p).astype(BF16) for p in (g0, g1, g2)]
    d = dest_ref[...]
    dts = [nt(eye, p.astype(F32).astype(BF16)).astype(BF16)
           for p in (d & 255, (d >> 8) & 255, d >> 16)]
    chunk_id = iota((NC, J), 0).astype(F32)
    tok_id = iota((LANES, J), 0).astype(F32)
    colsum = lambda a: jnp.sum(a, axis=0, keepdims=True)

    def block(jb):
        j0 = pl.multiple_of(jb * J, J)
        slot = (j0 + iota((1, J), 1)).astype(F32)
        before = cum_col <= slot
        c_j = colsum(before.astype(F32))
        rank = slot - colsum(jnp.where(before, tot_col, 0.0))
        onehot = (chunk_id == c_j).astype(BF16)
        within = colsum((jnp.dot(lt, onehot, preferred_element_type=F32) <= rank).astype(F32))
        tsel = tok_id == within
        pick = lambda a: colsum(jnp.where(tsel, jnp.dot(a, onehot, preferred_element_type=F32), 0.0))
        idx_ref[:, pl.ds(j0, J)] = (c_j * LANES + within).astype(jnp.int32)
        gsel_ref[:, pl.ds(j0, J)] = (pick(gts[0]) + pick(gts[1])) + pick(gts[2])
        dslot_ref[:, pl.ds(j0, J)] = (pick(dts[0]) + 256.0 * pick(dts[1]) + 65536.0 * pick(dts[2])).astype(jnp.int32)

    nblk = cap // J
    unroll = COMPACT_UNROLL if nblk % COMPACT_UNROLL == 0 else 1

    def blocks(i, carry):
        for u in range(unroll):
            block(i * unroll + u)
        return carry

    lax.fori_loop(0, nblk // unroll, blocks, 0)


def _compact(mask_t, gate_t, dest_t, cap):
    n = mask_t.shape[1]
    nc = n // LANES
    J = _row_tile(cap, COMPACT_BLOCK)
    chunked = lambda a: a.reshape(N_EXPERTS, nc, LANES)
    ispec = pl.BlockSpec((None, nc, LANES), lambda e: (e, 0, 0))
    ospec = pl.BlockSpec((None, 1, cap), lambda e: (e, 0, 0))
    oshape = lambda dt: jax.ShapeDtypeStruct((N_EXPERTS, 1, cap), dt)
    idx, gsel, dslot = pl.pallas_call(
        functools.partial(_compact_kernel, J),
        grid=(N_EXPERTS,),
        in_specs=[ispec, ispec, ispec],
        out_specs=[ospec, ospec, ospec],
        out_shape=[oshape(jnp.int32), oshape(F32), oshape(jnp.int32)],
        compiler_params=_cparams(("parallel",)),
        name="slot_lists",
    )(chunked(mask_t), chunked(gate_t), chunked(dest_t))
    return idx.reshape(-1), gsel.reshape(-1), dslot.reshape(-1)


def _sc_rows(data, idx, gather):
    R = idx.shape[0]
    D = data.shape[1]
    info = plsc.get_sparse_core_info()
    nw = info.num_cores * info.num_subcores
    per_w = R // nw
    assert per_w * nw == R and per_w % SC_WINDOW == 0
    mesh = plsc.VectorSubcoreMesh(core_axis_name="c", subcore_axis_name="s")

    @functools.partial(
        pl.kernel, mesh=mesh,
        out_type=jax.ShapeDtypeStruct((R, D), data.dtype),
        scratch_types=[pltpu.VMEM((SC_WINDOW,), jnp.int32),
                       pltpu.VMEM((SC_WINDOW, D), data.dtype),
                       pltpu.SemaphoreType.DMA],
        name="sc_gather_rows" if gather else "sc_scatter_rows",
    )
    def move(data_hbm, idx_hbm, out_hbm, idx_v, rows_v, sem):
        wid = lax.axis_index("s") * info.num_cores + lax.axis_index("c")

        @pl.loop(0, per_w // SC_WINDOW)
        def _(j):
            base = wid * per_w + j * SC_WINDOW
            pltpu.sync_copy(idx_hbm.at[pl.ds(base, SC_WINDOW)], idx_v)
            if gather:
                pltpu.async_copy(data_hbm.at[idx_v], rows_v, sem).wait()
                pltpu.sync_copy(rows_v, out_hbm.at[pl.ds(base, SC_WINDOW)])
            else:
                pltpu.sync_copy(data_hbm.at[pl.ds(base, SC_WINDOW)], rows_v)
                pltpu.async_copy(rows_v, out_hbm.at[idx_v], sem).wait()

    return move(data, idx)


def _gather_rows(table, idx):
    return _sc_rows(table, idx, True)


def _scatter_rows(src, idx):
    return _sc_rows(src, idx, False)


def _combine_kernel(n_rows, n_tiles, base_ref, x_ref, off_ref, end_ref, g_ref, b_ref, z_hbm, o_ref,
                    zbuf, sem, acc_ref):
    i = pl.program_id(0)

    def span(t):
        b0 = (base_ref[t] // ROW_ALIGN) * ROW_ALIGN
        return b0, (base_ref[t + 1] - b0 + Z_CHUNK - 1) // Z_CHUNK

    def row0(b0, k):
        return pl.multiple_of(jnp.minimum(b0 + k * Z_CHUNK, n_rows - Z_CHUNK), ROW_ALIGN)

    def copy(t, b0, k):
        slot = (t % 2) * Z_SLOTS + k % Z_SLOTS
        return pltpu.make_async_copy(z_hbm.at[pl.ds(row0(b0, k), Z_CHUNK)], zbuf.at[slot], sem.at[slot])

    def prime(t):
        b0, cnt = span(t)
        for s in range(Z_SLOTS):
            @pl.when(s < cnt)
            def _():
                copy(t, b0, s).start()

    @pl.when(i == 0)
    def _():
        prime(0)

    @pl.when(i + 1 < n_tiles)
    def _():
        prime(i + 1)

    base, nck = span(i)
    acc_ref[...] = jnp.zeros_like(acc_ref)
    off_col = off_ref[:, 0:1]
    end_col = end_ref[:, 0:1]
    lane = lax.broadcasted_iota(jnp.int32, (1, Z_CHUNK), 1)

    def body(k, carry):
        copy(i, base, k).wait()

        @pl.when((k >= 1) & (k - 1 + Z_SLOTS < nck))
        def _():
            copy(i, base, k - 1 + Z_SLOTS).start()

        d = row0(base, k) + lane
        own = (d >= off_col) & (d < end_col) & (d >= base + k * Z_CHUNK)
        z = _unpack_rows(zbuf[(i % 2) * Z_SLOTS + k % Z_SLOTS])
        acc_ref[...] += jnp.dot(own.astype(BF16), z, preferred_element_type=F32)
        return carry

    lax.fori_loop(0, nck, body, 0)

    y = ALPHA * x_ref[...] + acc_ref[...]
    mu = jnp.mean(y, axis=-1, keepdims=True)
    yc = y - mu
    var = jnp.mean(yc * yc, axis=-1, keepdims=True)
    o_ref[...] = yc * lax.rsqrt(var + LN_EPS) * g_ref[...] + b_ref[...]


def _combine_ln(x1, z, off, cnt, g, b):
    n = x1.shape[0]
    n_rows = z.shape[0]
    T = _row_tile(n, TOKEN_TILE)
    assert n_rows >= Z_CHUNK and n_rows % ROW_ALIGN == 0
    tile_base = jnp.concatenate([off[::T], jnp.full((1,), n_rows, jnp.int32)])
    offb = jnp.broadcast_to(off[:, None], (n, LANES))
    endb = jnp.broadcast_to((off + cnt)[:, None], (n, LANES))
    row = lambda w: pl.BlockSpec((T, w), lambda i, tb: (i, 0))
    vec = pl.BlockSpec((1, D_MODEL), lambda i, tb: (0, 0))
    return pl.pallas_call(
        functools.partial(_combine_kernel, n_rows, n // T),
        grid_spec=pltpu.PrefetchScalarGridSpec(
            num_scalar_prefetch=1,
            grid=(n // T,),
            in_specs=[row(D_MODEL), row(LANES), row(LANES), vec, vec, pl.BlockSpec(memory_space=pl.ANY)],
            out_specs=row(D_MODEL),
            scratch_shapes=[pltpu.VMEM((2 * Z_SLOTS, Z_CHUNK, PACKED), U32),
                            pltpu.SemaphoreType.DMA((2 * Z_SLOTS,)),
                            pltpu.VMEM((T, D_MODEL), F32)],
        ),
        out_shape=jax.ShapeDtypeStruct((n, D_MODEL), F32),
        compiler_params=_cparams(("arbitrary",)),
        name="combine_ln",
    )(tile_base, x1, offb, endb, g.reshape(1, -1), b.reshape(1, -1), z)


def _inproj1_kernel(x_ref, w_ref, tab_ref, gq_ref, gk_ref, seg_ref, q_ref, k_ref, v_ref):
    tm = x_ref.shape[0]
    sub = min(tm, MM_ROWS)
    sh = HEAD_DIM // 4
    seg = seg_ref[...]
    lane = lax.broadcasted_iota(jnp.int32, (1, LANES), 1)
    ones_col = jnp.where(lane == HEAD_DIM, 1.0, 0.0).astype(F32)
    for r in range(tm // sub):
        rows = slice(r * sub, (r + 1) * sub)
        h = jnp.dot(x_ref[rows, :].astype(BF16), w_ref[...], preferred_element_type=F32)

        def norm_rope(xc, gain, t0):
            ms = jnp.dot((xc * xc).astype(BF16), seg, preferred_element_type=F32)
            xn = xc * lax.rsqrt(ms + RMS_EPS) * gain
            return _rope_chunk(xn, tab_ref[t0, rows, :], tab_ref[t0 + 1, rows, :], tab_ref[t0 + 2, rows, :], sh)

        for c in range(C_Q // LANES):
            xc = h[:, c * LANES:(c + 1) * LANES]
            q_ref[rows, c * LANES:(c + 1) * LANES] = norm_rope(xc, gq_ref[...], 0).astype(BF16)
        for c in range(C_KV // LANES):
            xc = h[:, C_Q + c * LANES:C_Q + (c + 1) * LANES]
            kc = norm_rope(xc, gk_ref[...], 3).astype(BF16)
            k_ref[2 * c, rows, :] = kc[:, :HEAD_DIM]
            k_ref[2 * c + 1, rows, :] = kc[:, HEAD_DIM:]
            vc = h[:, C_Q + C_KV + c * LANES:C_Q + C_KV + (c + 1) * LANES]
            v_ref[2 * c, rows, :] = jnp.where(lane < HEAD_DIM, vc, ones_col).astype(BF16)
            v_ref[2 * c + 1, rows, :] = jnp.where(lane < HEAD_DIM, pltpu.roll(vc, HEAD_DIM, axis=1),
                                                  ones_col).astype(BF16)


def _inproj1(x, w, tab, gq, gk, S):
    n = x.shape[0]
    tm = _row_tile(S, 512)
    nt = S // tm
    lane_head = np.arange(LANES) // HEAD_DIM
    seg = jnp.asarray((lane_head[:, None] == lane_head[None, :]) / HEAD_DIM, BF16)
    tile2 = lambda g: jnp.concatenate([g, g]).reshape(1, LANES)
    const = lambda shape: pl.BlockSpec(shape, lambda i: (0,) * len(shape))
    return pl.pallas_call(
        _inproj1_kernel,
        grid=(n // tm,),
        in_specs=[
            pl.BlockSpec((tm, D_MODEL), lambda i: (i, 0)),
            const((D_MODEL, C_IN)),
            pl.BlockSpec((6, tm, LANES), lambda i: (0, i % nt, 0)),
            const((1, LANES)), const((1, LANES)), const((LANES, LANES)),
        ],
        out_specs=[
            pl.BlockSpec((tm, C_Q), lambda i: (i, 0)),
            pl.BlockSpec((AX_KV_HEADS, tm, HEAD_DIM), lambda i: (0, i, 0)),
            pl.BlockSpec((AX_KV_HEADS, tm, LANES), lambda i: (0, i, 0)),
        ],
        out_shape=[
            jax.ShapeDtypeStruct((n, C_Q), BF16),
            jax.ShapeDtypeStruct((AX_KV_HEADS, n, HEAD_DIM), BF16),
            jax.ShapeDtypeStruct((AX_KV_HEADS, n, LANES), BF16),
        ],
        compiler_params=_cparams(("parallel",)),
        name="inproj1",
    )(x, w, tab, tile2(gq), tile2(gk), seg)


def _axattn_kernel(q_ref, k_ref, v_ref, o_ref):
    G = AX_HEADS // AX_KV_HEADS
    TQ = q_ref.shape[0]
    chains = [(slice(r * AX_CHAIN_ROWS, (r + 1) * AX_CHAIN_ROWS), slice(j * HEAD_DIM, (j + 1) * HEAD_DIM))
              for r in range(TQ // AX_CHAIN_ROWS) for j in range(G)]

    def scores(c):
        rows, cols = chains[c]
        return lax.dot_general(q_ref[rows, cols], k_ref[...], (((1,), (1,)), ((), ())),
                               preferred_element_type=F32)

    s_next = scores(0)
    for c, (rows, cols) in enumerate(chains):
        s = s_next
        if c + 1 < len(chains):
            s_next = scores(c + 1)
        m = jnp.max(s, axis=-1, keepdims=True)
        p = jnp.exp2(s - m)
        o = jnp.dot(p.astype(BF16), v_ref[...], preferred_element_type=F32)
        o_ref[rows, cols] = (o[:, :HEAD_DIM] / o[:, HEAD_DIM:HEAD_DIM + 1]).astype(BF16)


def _axial_attention(q, k, v, B, S):
    n = q.shape[0]
    TQ = _row_tile(S, AX_Q_TILE)
    nq = S // TQ
    G = AX_HEADS // AX_KV_HEADS
    qspec = pl.BlockSpec((TQ, G * HEAD_DIM), lambda b, g, i: (b * nq + i, g))
    kspec = pl.BlockSpec((None, None, S, HEAD_DIM), lambda b, g, i: (g, b, 0, 0))
    vspec = pl.BlockSpec((None, None, S, LANES), lambda b, g, i: (g, b, 0, 0))
    return pl.pallas_call(
        _axattn_kernel,
        grid=(B, AX_KV_HEADS, nq),
        in_specs=[qspec, kspec, vspec],
        out_specs=qspec,
        out_shape=jax.ShapeDtypeStruct((n, C_Q), BF16),
        compiler_params=_cparams(("parallel", "parallel", "arbitrary")),
        name="axial_attention",
    )(q, k, v)


def _moe(x1, xb, aff, wg, wu, wd, layer, ln_g, ln_b, n):
    cap = CAPACITY_FACTOR * n // N_EXPERTS
    gate_t, mask_t = _expert_choice(aff.T, cap)
    cnt = jnp.sum(mask_t, axis=0)
    off = jnp.cumsum(cnt) - cnt
    dest_t = off[None, :] + jnp.cumsum(mask_t, axis=0) - mask_t
    idx, gsel, dslot = _compact(mask_t, gate_t, dest_t, cap)
    xe = _gather_rows(xb, idx)
    gb = jnp.broadcast_to(gsel[:, None], (gsel.shape[0], LANES))
    ye = _expert_ffn(xe, gb, wg, wu, wd, layer, cap)
    z = _scatter_rows(ye, dslot)
    return _combine_ln(x1, z, off, cnt, ln_g, ln_b)


def _trunk(x, p):
    B, S, _ = x.shape
    n = B * S
    xf = x.reshape(n, D_MODEL)
    tab_w = _tables_window(S)
    tab_a = _tables_axial(S)
    for layer in range(DEPTH):
        i = layer // 2
        if layer % 2 == 0:
            ug, q, k, v = _inproj0(xf, p['ab_w_in'][i], tab_w, S)
            conv = _conv_branch(ug, p['ab_conv_w'][i], p['ab_conv_b'][i], p['ab_conv_ln_g'][i],
                                p['ab_conv_ln_b'][i], S)
            att = _window_attention(q, k, v, p['ab_sink'][i], B, S)
            acts, w_out = [conv, att], p['ab_w_out'][i]
        else:
            q, k, v = _inproj1(xf, p['c_w_in'][i], tab_a, p['c_q_norm'][i], p['c_k_norm'][i], S)
            k = k.reshape(AX_KV_HEADS, B, S, HEAD_DIM)
            v = v.reshape(AX_KV_HEADS, B, S, LANES)
            acts, w_out = [_axial_attention(q, k, v, B, S)], p['c_w_out'][i]
        x1, xb, aff = _outproj_ln_router(acts, xf, w_out, p['ln1_g'][layer], p['ln1_b'][layer],
                                         p['moe_router'][layer])
        xf = _moe(x1, xb, aff, p['moe_w_gate'], p['moe_w_up'], p['moe_w_down'], layer,
                  p['ln2_g'][layer], p['ln2_b'][layer], n)
    return xf.reshape(B, S, D_MODEL)


def kernel(x_prompt, x_sample, ab_w_in, ab_conv_w, ab_conv_b, ab_conv_ln_g, ab_conv_ln_b, ab_sink, ab_w_out, c_w_in, c_q_norm, c_k_norm, c_w_out, ln1_g, ln1_b, moe_router, moe_w_gate, moe_w_up, moe_w_down, ln2_g, ln2_b):
    p = dict(
        ab_w_in=ab_w_in.astype(BF16), ab_conv_w=ab_conv_w, ab_conv_b=ab_conv_b,
        ab_conv_ln_g=ab_conv_ln_g, ab_conv_ln_b=ab_conv_ln_b, ab_sink=ab_sink,
        ab_w_out=ab_w_out.astype(BF16), c_w_in=c_w_in.astype(BF16), c_q_norm=c_q_norm,
        c_k_norm=c_k_norm, c_w_out=c_w_out.astype(BF16), ln1_g=ln1_g, ln1_b=ln1_b,
        moe_router=moe_router, moe_w_gate=moe_w_gate.astype(BF16), moe_w_up=moe_w_up.astype(BF16),
        moe_w_down=moe_w_down.astype(BF16), ln2_g=ln2_g, ln2_b=ln2_b,
    )
    return (_trunk(x_prompt, p), _trunk(x_sample, p))
```

```python
import functools

import jax
import jax.numpy as jnp
import numpy as np
from jax import lax
from jax.experimental import pallas as pl
from jax.experimental.pallas import tpu as pltpu
from jax.experimental.pallas import tpu_sc as plsc

D_MODEL = 1024
DEPTH = 2
GRID_W = 64
BLOCK = 128
HEAD_DIM = 64
CONV_DIM = 512
CONV_WIDTH = 31
WIN_HEADS = 8
WIN_KV_HEADS = 2
WINDOW = 128
ROT_DIM_B = 16
ROPE_THETA_B = 500000.0
AX_HEADS = 16
AX_KV_HEADS = 4
ROPE_THETA_C = 10000.0
N_EXPERTS = 16
CAPACITY_FACTOR = 2
ALPHA = (2 * DEPTH) ** 0.25
AB_Q = WIN_HEADS * HEAD_DIM
AB_KV = WIN_KV_HEADS * HEAD_DIM
AB_IN = 2 * CONV_DIM + AB_Q + 2 * AB_KV
C_Q = AX_HEADS * HEAD_DIM
C_KV = AX_KV_HEADS * HEAD_DIM
C_IN = C_Q + 2 * C_KV
LN_EPS = 1e-5
LOG2E = 1.4426950408889634
RMS_EPS = 1e-6

LANES = 128
SUBLANES = 8
CONV_ROWS = 64
CONV_SPAN = 24
MM_ROWS = 128
PROJ_TILE = 1024
CONV_HALO = 16
VMEM_LIMIT = 48 * 1024 * 1024

TOKEN_TILE = 256
Z_CHUNK = 256
Z_SLOTS = 4
ROW_ALIGN = 16
SC_WINDOW = 64
COMPACT_BLOCK = 256
COMPACT_UNROLL = 4
AX_Q_TILE = 1024
AX_CHAIN_ROWS = 512

BF16 = jnp.bfloat16
F32 = jnp.float32
U32 = jnp.uint32
PACKED = D_MODEL // 2


def _pack_rows(x):
    b = lax.bitcast_convert_type(x.astype(BF16).astype(F32), U32)
    return (b[:, :PACKED] >> 16) | (b[:, PACKED:] & jnp.uint32(0xFFFF0000))


def _unpack_rows(w):
    lo = lax.bitcast_convert_type(w << 16, F32)
    hi = lax.bitcast_convert_type(w & jnp.uint32(0xFFFF0000), F32)
    return jnp.concatenate([lo, hi], axis=1).astype(BF16)


def _cparams(sem):
    return pltpu.CompilerParams(dimension_semantics=sem, vmem_limit_bytes=VMEM_LIMIT)


def _row_tile(n, pref):
    t = min(n, pref)
    assert n % t == 0
    return t


def _tables(S, groups, q_scale):
    lane = np.arange(LANES) % HEAD_DIM
    ang = jnp.zeros((S, LANES), F32)
    first = np.zeros(LANES, bool)
    second = np.zeros(LANES, bool)
    for lo, shift, theta, pos in groups:
        dim = 2 * shift
        inv = theta ** (-jnp.arange(0, dim, 2, dtype=F32) / dim)
        member = (lane >= lo) & (lane < lo + dim)
        inv_lane = jnp.where(member, inv[np.where(member, (lane - lo) % shift, 0)], 0.0)
        ang = ang + pos[:, None] * inv_lane[None, :]
        first |= member & (lane - lo < shift)
        second |= member & (lane - lo >= shift)
    cos, sin = jnp.cos(ang), jnp.sin(ang)
    sa = jnp.where(first, -sin, 0.0)
    sb = jnp.where(second, sin, 0.0)
    return jnp.stack([cos * q_scale, sa * q_scale, sb * q_scale, cos, sa, sb])


def _tables_window(S):
    pos = jnp.arange(S, dtype=F32)
    return _tables(S, [(0, ROT_DIM_B // 2, ROPE_THETA_B, pos)], HEAD_DIM ** -0.5)


def _tables_axial(S):
    rows = S // GRID_W
    gr, gc = jnp.meshgrid(jnp.arange(rows), jnp.arange(GRID_W), indexing='ij')
    groups = [(0, HEAD_DIM // 4, ROPE_THETA_C, gr.reshape(-1).astype(F32)),
              (HEAD_DIM // 2, HEAD_DIM // 4, ROPE_THETA_C, gc.reshape(-1).astype(F32))]
    return _tables(S, groups, HEAD_DIM ** -0.5 * LOG2E)


def _rope_chunk(x, c, sa, sb, shift):
    fwd = pltpu.roll(x, LANES - shift, axis=1)
    bwd = pltpu.roll(x, shift, axis=1)
    return x * c + fwd * sa + bwd * sb


def _inproj0_kernel(x_ref, w_ref, tab_ref, ug_ref, q_ref, k_ref, v_ref):
    tm = x_ref.shape[0]
    sub = min(tm, MM_ROWS)
    sh = ROT_DIM_B // 2
    q0 = 2 * CONV_DIM
    k0 = q0 + AB_Q
    for r in range(tm // sub):
        rows = slice(r * sub, (r + 1) * sub)
        tab = lambda t: tab_ref[t, rows, :]
        h = jnp.dot(x_ref[rows, :].astype(BF16), w_ref[...], preferred_element_type=F32)
        ug_ref[rows, :] = h[:, :q0].astype(BF16)
        for c in range(AB_Q // LANES):
            xc = h[:, q0 + c * LANES:q0 + (c + 1) * LANES]
            q_ref[rows, c * LANES:(c + 1) * LANES] = _rope_chunk(xc, tab(0), tab(1), tab(2), sh).astype(BF16)
        for c in range(AB_KV // LANES):
            xc = h[:, k0 + c * LANES:k0 + (c + 1) * LANES]
            k_ref[rows, c * LANES:(c + 1) * LANES] = _rope_chunk(xc, tab(3), tab(4), tab(5), sh).astype(BF16)
        v_ref[rows, :] = h[:, k0 + AB_KV:].astype(BF16)


def _inproj0(x, w, tab, S):
    n = x.shape[0]
    tm = _row_tile(S, PROJ_TILE)
    nt = S // tm
    return pl.pallas_call(
        _inproj0_kernel,
        grid=(n // tm,),
        in_specs=[
            pl.BlockSpec((tm, D_MODEL), lambda i: (i, 0)),
            pl.BlockSpec((D_MODEL, AB_IN), lambda i: (0, 0)),
            pl.BlockSpec((6, tm, LANES), lambda i: (0, i % nt, 0)),
        ],
        out_specs=[
            pl.BlockSpec((tm, 2 * CONV_DIM), lambda i: (i, 0)),
            pl.BlockSpec((tm, AB_Q), lambda i: (i, 0)),
            pl.BlockSpec((tm, AB_KV), lambda i: (i, 0)),
            pl.BlockSpec((tm, AB_KV), lambda i: (i, 0)),
        ],
        out_shape=[
            jax.ShapeDtypeStruct((n, 2 * CONV_DIM), BF16),
            jax.ShapeDtypeStruct((n, AB_Q), BF16),
            jax.ShapeDtypeStruct((n, AB_KV), BF16),
            jax.ShapeDtypeStruct((n, AB_KV), BF16),
        ],
        compiler_params=_cparams(("parallel",)),
        name="inproj0",
    )(x, w, tab)


def _conv_kernel(nt, cur_ref, prev_ref, next_ref, w_ref, b_ref, g_ref, beta_ref, o_ref, buf_ref, shift_ref, hc_ref):
    T = cur_ref.shape[0]
    si = pl.program_id(0) % nt

    def glu(blk):
        u = blk[:, :CONV_DIM].astype(F32)
        gt = blk[:, CONV_DIM:].astype(F32)
        return u * jax.nn.sigmoid(gt)

    pv = jnp.where(si > 0, 1.0, 0.0).astype(F32)
    nv = jnp.where(si < nt - 1, 1.0, 0.0).astype(F32)
    buf_ref[0:CONV_HALO, :] = glu(prev_ref[...]) * pv
    buf_ref[CONV_HALO:CONV_HALO + T, :] = glu(cur_ref[...])
    buf_ref[CONV_HALO + T:, :] = glu(next_ref[...]) * nv
    L = shift_ref.shape[1]
    for r in range(1, SUBLANES):
        shift_ref[r - 1] = buf_ref[r:r + L, :]

    off = CONV_HALO - CONV_WIDTH // 2

    def rows(rb, carry):
        r0 = pl.multiple_of(rb * CONV_ROWS, CONV_ROWS)
        for c in range(CONV_DIM // LANES):
            cols = slice(c * LANES, (c + 1) * LANES)
            acc = jnp.zeros((CONV_ROWS, LANES), F32)
            for r in range(SUBLANES):
                src = buf_ref if r == 0 else shift_ref.at[r - 1]
                xr = src[pl.ds(r0, CONV_ROWS + CONV_SPAN), cols]
                for a in range(CONV_SPAN // SUBLANES + 1):
                    j = SUBLANES * a + r - off
                    if 0 <= j < CONV_WIDTH:
                        wj = w_ref[j * SUBLANES:(j + 1) * SUBLANES, cols]
                        wt = jnp.concatenate([wj] * (CONV_ROWS // SUBLANES), axis=0)
                        acc = acc + xr[SUBLANES * a:SUBLANES * a + CONV_ROWS] * wt
            hc_ref[pl.ds(r0, CONV_ROWS), cols] = acc + b_ref[:, cols]
        return carry

    lax.fori_loop(0, T // CONV_ROWS, rows, 0)
    hcv = hc_ref[...]
    mu = jnp.mean(hcv, axis=-1, keepdims=True)
    xc = hcv - mu
    var = jnp.mean(xc * xc, axis=-1, keepdims=True)
    y = xc * lax.rsqrt(var + LN_EPS) * g_ref[...] + beta_ref[...]
    o_ref[...] = (y * jax.nn.sigmoid(y)).astype(BF16)


def _conv_branch(ug, w, b, g, beta, S):
    n = ug.shape[0]
    T = _row_tile(S, 512)
    nt = S // T
    r = T // CONV_HALO
    nh = n // CONV_HALO
    vec = lambda: pl.BlockSpec((1, CONV_DIM), lambda i: (0, 0))
    return pl.pallas_call(
        functools.partial(_conv_kernel, nt),
        grid=(n // T,),
        in_specs=[
            pl.BlockSpec((T, 2 * CONV_DIM), lambda i: (i, 0)),
            pl.BlockSpec((CONV_HALO, 2 * CONV_DIM), lambda i: (jnp.maximum(i * r - 1, 0), 0)),
            pl.BlockSpec((CONV_HALO, 2 * CONV_DIM), lambda i: (jnp.minimum((i + 1) * r, nh - 1), 0)),
            pl.BlockSpec((CONV_WIDTH * SUBLANES, CONV_DIM), lambda i: (0, 0)),
            vec(), vec(), vec(),
        ],
        out_specs=pl.BlockSpec((T, CONV_DIM), lambda i: (i, 0)),
        out_shape=jax.ShapeDtypeStruct((n, CONV_DIM), BF16),
        scratch_shapes=[pltpu.VMEM((T + 2 * CONV_HALO, CONV_DIM), F32),
                        pltpu.VMEM((SUBLANES - 1, T + 2 * CONV_HALO - SUBLANES, CONV_DIM), F32),
                        pltpu.VMEM((T, CONV_DIM), F32)],
        compiler_params=_cparams(("parallel",)),
        name="conv_branch",
    )(ug, ug, ug, jnp.repeat(w, SUBLANES, axis=0), b.reshape(1, -1), g.reshape(1, -1), beta.reshape(1, -1))


def _winattn_kernel(nq, sink_ref, q_ref, kp_ref, kc_ref, kn_ref, vp_ref, vc_ref, vn_ref, o_ref):
    qi = pl.program_id(1)
    G = WIN_HEADS // WIN_KV_HEADS
    M = G * BLOCK
    row = lax.broadcasted_iota(jnp.int32, (M, 3 * BLOCK), 0) % BLOCK
    col = lax.broadcasted_iota(jnp.int32, (M, 3 * BLOCK), 1)
    band = jnp.abs(col - BLOCK - row) <= WINDOW
    head = lax.broadcasted_iota(jnp.int32, (M, 1), 0) // BLOCK
    chains = [(half, g) for half in range(2) for g in range(WIN_KV_HEADS)]

    def operands(half, g):
        sl = slice(g * HEAD_DIM, (g + 1) * HEAD_DIM)
        krefs, vrefs = ((kp_ref, kc_ref), (vp_ref, vc_ref)) if half == 0 else ((kc_ref, kn_ref), (vc_ref, vn_ref))
        return (jnp.concatenate([r[:, sl] for r in krefs], axis=0),
                jnp.concatenate([r[:, sl] for r in vrefs], axis=0))

    def scores(half, g):
        rows = slice(half * BLOCK, (half + 1) * BLOCK)
        qs = jnp.concatenate([q_ref[rows, h * HEAD_DIM:(h + 1) * HEAD_DIM] for h in range(g * G, (g + 1) * G)],
                             axis=0)
        return lax.dot_general(qs, operands(half, g)[0], (((1,), (1,)), ((), ())), preferred_element_type=F32)

    s_next = scores(*chains[0])
    for c, (half, g) in enumerate(chains):
        rows = slice(half * BLOCK, (half + 1) * BLOCK)
        s = s_next
        if c + 1 < len(chains):
            s_next = scores(*chains[c + 1])
        if half == 0:
            valid = band & ((qi > 0) | (col >= BLOCK))
        else:
            valid = band & ((qi < nq - 1) | (col < 2 * BLOCK))
        heads = range(g * G, (g + 1) * G)
        s = jnp.where(valid, s, -jnp.inf)
        sk = jnp.zeros((M, 1), F32)
        for j, h in enumerate(heads):
            sk = jnp.where(head == j, sink_ref[h], sk)
        m = jnp.maximum(jnp.max(s, axis=-1, keepdims=True), sk)
        p = jnp.exp(s - m)
        den = jnp.sum(p, axis=-1, keepdims=True) + jnp.exp(sk - m)
        o = jnp.dot(p.astype(BF16), operands(half, g)[1], preferred_element_type=F32) / den
        for j, h in enumerate(heads):
            o_ref[rows, h * HEAD_DIM:(h + 1) * HEAD_DIM] = o[j * BLOCK:(j + 1) * BLOCK].astype(BF16)


def _window_attention(q, k, v, sink, B, S):
    n = q.shape[0]
    nb = S // BLOCK
    nq = nb // 2
    cur = lambda b, i: (b * nq + i, 0)
    prev = lambda b, i: (b * nb + jnp.maximum(2 * i - 1, 0), 0)
    nxt = lambda b, i: (b * nb + jnp.minimum(2 * i + 2, nb - 1), 0)
    edge = lambda f: pl.BlockSpec((BLOCK, AB_KV), f)
    mid = pl.BlockSpec((2 * BLOCK, AB_KV), cur)
    return pl.pallas_call(
        functools.partial(_winattn_kernel, nq),
        grid=(B, nq),
        in_specs=[
            pl.BlockSpec(memory_space=pltpu.SMEM),
            pl.BlockSpec((2 * BLOCK, AB_Q), cur),
            edge(prev), mid, edge(nxt), edge(prev), mid, edge(nxt),
        ],
        out_specs=pl.BlockSpec((2 * BLOCK, AB_Q), cur),
        out_shape=jax.ShapeDtypeStruct((n, AB_Q), BF16),
        compiler_params=_cparams(("parallel", "parallel")),
        name="window_attention",
    )(sink, q, k, k, k, v, v, v)


def _outproj_kernel(n_act, *refs):
    acts = refs[:n_act]
    x_ref, w_ref, g_ref, b_ref, rcat_ref, rhi_ref, xo_ref, xb_ref, aff_ref = refs[n_act:]
    tm = x_ref.shape[0]
    sub = min(tm, MM_ROWS)
    def project(r):
        rows = slice(r * sub, (r + 1) * sub)
        a = acts[0][rows, :] if n_act == 1 else jnp.concatenate([ref[rows, :] for ref in acts], axis=-1)
        return jnp.dot(a, w_ref[...], preferred_element_type=F32)

    mix_next = project(0)
    for r in range(tm // sub):
        rows = slice(r * sub, (r + 1) * sub)
        mix = mix_next
        if r + 1 < tm // sub:
            mix_next = project(r + 1)
        y = ALPHA * x_ref[rows, :] + mix
        mu = jnp.mean(y, axis=-1, keepdims=True)
        yc = y - mu
        var = jnp.mean(yc * yc, axis=-1, keepdims=True)
        x1 = yc * lax.rsqrt(var + LN_EPS) * g_ref[...] + b_ref[...]
        xo_ref[rows, :] = x1
        xb_ref[rows, :] = _pack_rows(x1)
        x_hi = x1.astype(BF16)
        x_lo = (x1 - x_hi.astype(F32)).astype(BF16)
        t = jnp.dot(x_hi, rcat_ref[...], preferred_element_type=F32)
        logits = (t[:, :N_EXPERTS] + jnp.dot(x_lo, rhi_ref[...], preferred_element_type=F32)) + t[:, N_EXPERTS:]
        m = jnp.max(logits, axis=-1, keepdims=True)
        e = jnp.exp(logits - m)
        aff_ref[rows, :] = e / jnp.sum(e, axis=-1, keepdims=True)


def _outproj_ln_router(acts, x, w, g, b, w_r):
    n = x.shape[0]
    tm = _row_tile(n, PROJ_TILE)
    r_hi = w_r.astype(BF16)
    r_lo = (w_r - r_hi.astype(F32)).astype(BF16)
    row = lambda width: pl.BlockSpec((tm, width), lambda i: (i, 0))
    const = lambda shape: pl.BlockSpec(shape, lambda i: (0, 0))
    return pl.pallas_call(
        functools.partial(_outproj_kernel, len(acts)),
        grid=(n // tm,),
        in_specs=[row(a.shape[1]) for a in acts] + [
            row(D_MODEL), const((D_MODEL, D_MODEL)), const((1, D_MODEL)), const((1, D_MODEL)),
            const((D_MODEL, 2 * N_EXPERTS)), const((D_MODEL, N_EXPERTS)),
        ],
        out_specs=[row(D_MODEL), row(PACKED), row(N_EXPERTS)],
        out_shape=[
            jax.ShapeDtypeStruct((n, D_MODEL), F32),
            jax.ShapeDtypeStruct((n, PACKED), U32),
            jax.ShapeDtypeStruct((n, N_EXPERTS), F32),
        ],
        compiler_params=_cparams(("parallel",)),
        name="outproj_ln_router",
    )(*acts, x, w, g.reshape(1, -1), b.reshape(1, -1), jnp.concatenate([r_hi, r_lo], axis=1), r_hi)


def _topk_kernel(cap, aff_ref, gate_ref, mask_ref):
    n = aff_ref.shape[1]
    bits = pltpu.bitcast(aff_ref[...], jnp.int32)
    count = lambda m: jnp.sum(m.astype(jnp.int32), axis=1, keepdims=True)

    def value_step(i, t):
        cand = t | (jnp.int32(1) << (30 - i))
        return jnp.where(count(bits >= cand) >= cap, cand, t)

    thr = lax.fori_loop(0, 31, value_step, jnp.zeros((N_EXPERTS, 1), jnp.int32))
    gt = bits > thr
    eq = bits == thr
    need = cap - count(gt)
    idx = lax.broadcasted_iota(jnp.int32, (N_EXPERTS, n), 1)
    nbits = max(int(n).bit_length(), 1)

    def index_step(i, j0):
        cand = j0 | (jnp.int32(1) << (nbits - 1 - i))
        return jnp.where(count(eq & (idx < cand)) < need, cand, j0)

    j0 = lax.fori_loop(0, nbits, index_step, jnp.zeros((N_EXPERTS, 1), jnp.int32))
    sel = gt | (eq & (idx <= j0))
    gate_ref[...] = jnp.where(sel, aff_ref[...], 0.0)
    mask_ref[...] = sel.astype(jnp.int32)


def _expert_choice(aff_t, cap):
    n = aff_t.shape[1]
    return pl.pallas_call(
        functools.partial(_topk_kernel, cap),
        out_shape=[
            jax.ShapeDtypeStruct((N_EXPERTS, n), F32),
            jax.ShapeDtypeStruct((N_EXPERTS, n), jnp.int32),
        ],
        compiler_params=pltpu.CompilerParams(vmem_limit_bytes=VMEM_LIMIT),
        name="expert_choice_topk",
    )(aff_t)


def _ffn_kernel(x_ref, g_ref, wg_ref, wu_ref, wd_ref, o_ref):
    x = _unpack_rows(x_ref[...])
    hg = jnp.dot(x, wg_ref[...], preferred_element_type=F32)
    hu = jnp.dot(x, wu_ref[...], preferred_element_type=F32)
    h = (hg * jax.nn.sigmoid(hg) * hu).astype(BF16)
    y = jnp.dot(h, wd_ref[...], preferred_element_type=F32)
    o_ref[...] = _pack_rows(y * g_ref[:, 0:1])


def _expert_ffn(xe, gate, wg, wu, wd, layer, cap):
    rows = xe.shape[0]
    tm = _row_tile(cap, 1024)
    nr = cap // tm
    wspec = pl.BlockSpec((None, None, D_MODEL, D_MODEL), lambda e, r: (layer, e, 0, 0))
    return pl.pallas_call(
        _ffn_kernel,
        grid=(N_EXPERTS, nr),
        in_specs=[
            pl.BlockSpec((tm, PACKED), lambda e, r: (e * nr + r, 0)),
            pl.BlockSpec((tm, LANES), lambda e, r: (e * nr + r, 0)),
            wspec, wspec, wspec,
        ],
        out_specs=pl.BlockSpec((tm, PACKED), lambda e, r: (e * nr + r, 0)),
        out_shape=jax.ShapeDtypeStruct((rows, PACKED), U32),
        compiler_params=_cparams(("parallel", "arbitrary")),
        name="expert_ffn",
    )(xe, gate, wg, wu, wd)


def _compact_kernel(J, mask_ref, gate_ref, dest_ref, idx_ref, gsel_ref, dslot_ref):
    NC = mask_ref.shape[0]
    cap = idx_ref.shape[1]
    nt = lambda a, b: lax.dot_general(a, b, (((1,), (1,)), ((), ())), preferred_element_type=F32)
    iota = lambda shape, d: lax.broadcasted_iota(jnp.int32, shape, d)
    m = mask_ref[...].astype(F32).astype(BF16)
    tril = (iota((LANES, LANES), 1) <= iota((LANES, LANES), 0)).astype(BF16)
    eye = (iota((LANES, LANES), 1) == iota((LANES, LANES), 0)).astype(BF16)
    lt = nt(tril, m).astype(BF16)
    tot_b = jnp.dot(m, jnp.ones((LANES, LANES), BF16), preferred_element_type=F32)
    trilc = (iota((NC, NC), 1) <= iota((NC, NC), 0)).astype(BF16)
    cum_col = jnp.dot(trilc, tot_b.astype(BF16), preferred_element_type=F32)[:, 0:1]
    tot_col = tot_b[:, 0:1]
    g = gate_ref[...]
    g0 = g.astype(BF16)
    r1 = g - g0.astype(F32)
    g1 = r1.astype(BF16)
    g2 = (r1 - g1.astype(F32)).astype(BF16)
    gts = [nt(eye, p).astype(BF16) for p in (g0, g1, g2)]
    d = dest_ref[...]
    dts = [nt(eye, p.astype(F32).astype(BF16)).astype(BF16)
           for p in (d & 255, (d >> 8) & 255, d >> 16)]
    chunk_id = iota((NC, J), 0).astype(F32)
    tok_id = iota((LANES, J), 0).astype(F32)
    colsum = lambda a: jnp.sum(a, axis=0, keepdims=True)

    def block(jb):
        j0 = pl.multiple_of(jb * J, J)
        slot = (j0 + iota((1, J), 1)).astype(F32)
        before = cum_col <= slot
        c_j = colsum(before.astype(F32))
        rank = slot - colsum(jnp.where(before, tot_col, 0.0))
        onehot = (chunk_id == c_j).astype(BF16)
        within = colsum((jnp.dot(lt, onehot, preferred_element_type=F32) <= rank).astype(F32))
        tsel = tok_id == within
        pick = lambda a: colsum(jnp.where(tsel, jnp.dot(a, onehot, preferred_element_type=F32), 0.0))
        idx_ref[:, pl.ds(j0, J)] = (c_j * LANES + within).astype(jnp.int32)
        gsel_ref[:, pl.ds(j0, J)] = (pick(gts[0]) + pick(gts[1])) + pick(gts[2])
        dslot_ref[:, pl.ds(j0, J)] = (pick(dts[0]) + 256.0 * pick(dts[1]) + 65536.0 * pick(dts[2])).astype(jnp.int32)

    nblk = cap // J
    unroll = COMPACT_UNROLL if nblk % COMPACT_UNROLL == 0 else 1

    def blocks(i, carry):
        for u in range(unroll):
            block(i * unroll + u)
        return carry

    lax.fori_loop(0, nblk // unroll, blocks, 0)


def _compact(mask_t, gate_t, dest_t, cap):
    n = mask_t.shape[1]
    nc = n // LANES
    J = _row_tile(cap, COMPACT_BLOCK)
    chunked = lambda a: a.reshape(N_EXPERTS, nc, LANES)
    ispec = pl.BlockSpec((None, nc, LANES), lambda e: (e, 0, 0))
    ospec = pl.BlockSpec((None, 1, cap), lambda e: (e, 0, 0))
    oshape = lambda dt: jax.ShapeDtypeStruct((N_EXPERTS, 1, cap), dt)
    idx, gsel, dslot = pl.pallas_call(
        functools.partial(_compact_kernel, J),
        grid=(N_EXPERTS,),
        in_specs=[ispec, ispec, ispec],
        out_specs=[ospec, ospec, ospec],
        out_shape=[oshape(jnp.int32), oshape(F32), oshape(jnp.int32)],
        compiler_params=_cparams(("parallel",)),
        name="slot_lists",
    )(chunked(mask_t), chunked(gate_t), chunked(dest_t))
    return idx.reshape(-1), gsel.reshape(-1), dslot.reshape(-1)


def _sc_rows(data, idx, gather):
    R = idx.shape[0]
    D = data.shape[1]
    info = plsc.get_sparse_core_info()
    nw = info.num_cores * info.num_subcores
    per_w = R // nw
    assert per_w * nw == R and per_w % SC_WINDOW == 0
    mesh = plsc.VectorSubcoreMesh(core_axis_name="c", subcore_axis_name="s")

    @functools.partial(
        pl.kernel, mesh=mesh,
        out_type=jax.ShapeDtypeStruct((R, D), data.dtype),
        scratch_types=[pltpu.VMEM((SC_WINDOW,), jnp.int32),
                       pltpu.VMEM((SC_WINDOW, D), data.dtype),
                       pltpu.SemaphoreType.DMA],
        name="sc_gather_rows" if gather else "sc_scatter_rows",
    )
    def move(data_hbm, idx_hbm, out_hbm, idx_v, rows_v, sem):
        wid = lax.axis_index("s") * info.num_cores + lax.axis_index("c")

        @pl.loop(0, per_w // SC_WINDOW)
        def _(j):
            base = wid * per_w + j * SC_WINDOW
            pltpu.sync_copy(idx_hbm.at[pl.ds(base, SC_WINDOW)], idx_v)
            if gather:
                pltpu.async_copy(data_hbm.at[idx_v], rows_v, sem).wait()
                pltpu.sync_copy(rows_v, out_hbm.at[pl.ds(base, SC_WINDOW)])
            else:
                pltpu.sync_copy(data_hbm.at[pl.ds(base, SC_WINDOW)], rows_v)
                pltpu.async_copy(rows_v, out_hbm.at[idx_v], sem).wait()

    return move(data, idx)


def _gather_rows(table, idx):
    return _sc_rows(table, idx, True)


def _scatter_rows(src, idx):
    return _sc_rows(src, idx, False)


def _combine_kernel(n_rows, n_tiles, base_ref, x_ref, off_ref, end_ref, g_ref, b_ref, z_hbm, o_ref,
                    zbuf, sem, acc_ref):
    i = pl.program_id(0)

    def span(t):
        b0 = (base_ref[t] // ROW_ALIGN) * ROW_ALIGN
        return b0, (base_ref[t + 1] - b0 + Z_CHUNK - 1) // Z_CHUNK

    def row0(b0, k):
        return pl.multiple_of(jnp.minimum(b0 + k * Z_CHUNK, n_rows - Z_CHUNK), ROW_ALIGN)

    def copy(t, b0, k):
        slot = (t % 2) * Z_SLOTS + k % Z_SLOTS
        return pltpu.make_async_copy(z_hbm.at[pl.ds(row0(b0, k), Z_CHUNK)], zbuf.at[slot], sem.at[slot])

    def prime(t):
        b0, cnt = span(t)
        for s in range(Z_SLOTS):
            @pl.when(s < cnt)
            def _():
                copy(t, b0, s).start()

    @pl.when(i == 0)
    def _():
        prime(0)

    @pl.when(i + 1 < n_tiles)
    def _():
        prime(i + 1)

    base, nck = span(i)
    acc_ref[...] = jnp.zeros_like(acc_ref)
    off_col = off_ref[:, 0:1]
    end_col = end_ref[:, 0:1]
    lane = lax.broadcasted_iota(jnp.int32, (1, Z_CHUNK), 1)

    def body(k, carry):
        copy(i, base, k).wait()

        @pl.when((k >= 1) & (k - 1 + Z_SLOTS < nck))
        def _():
            copy(i, base, k - 1 + Z_SLOTS).start()

        d = row0(base, k) + lane
        own = (d >= off_col) & (d < end_col) & (d >= base + k * Z_CHUNK)
        z = _unpack_rows(zbuf[(i % 2) * Z_SLOTS + k % Z_SLOTS])
        acc_ref[...] += jnp.dot(own.astype(BF16), z, preferred_element_type=F32)
        return carry

    lax.fori_loop(0, nck, body, 0)

    y = ALPHA * x_ref[...] + acc_ref[...]
    mu = jnp.mean(y, axis=-1, keepdims=True)
    yc = y - mu
    var = jnp.mean(yc * yc, axis=-1, keepdims=True)
    o_ref[...] = yc * lax.rsqrt(var + LN_EPS) * g_ref[...] + b_ref[...]


def _combine_ln(x1, z, off, cnt, g, b):
    n = x1.shape[0]
    n_rows = z.shape[0]
    T = _row_tile(n, TOKEN_TILE)
    assert n_rows >= Z_CHUNK and n_rows % ROW_ALIGN == 0
    tile_base = jnp.concatenate([off[::T], jnp.full((1,), n_rows, jnp.int32)])
    offb = jnp.broadcast_to(off[:, None], (n, LANES))
    endb = jnp.broadcast_to((off + cnt)[:, None], (n, LANES))
    row = lambda w: pl.BlockSpec((T, w), lambda i, tb: (i, 0))
    vec = pl.BlockSpec((1, D_MODEL), lambda i, tb: (0, 0))
    return pl.pallas_call(
        functools.partial(_combine_kernel, n_rows, n // T),
        grid_spec=pltpu.PrefetchScalarGridSpec(
            num_scalar_prefetch=1,
            grid=(n // T,),
            in_specs=[row(D_MODEL), row(LANES), row(LANES), vec, vec, pl.BlockSpec(memory_space=pl.ANY)],
            out_specs=row(D_MODEL),
            scratch_shapes=[pltpu.VMEM((2 * Z_SLOTS, Z_CHUNK, PACKED), U32),
                            pltpu.SemaphoreType.DMA((2 * Z_SLOTS,)),
                            pltpu.VMEM((T, D_MODEL), F32)],
        ),
        out_shape=jax.ShapeDtypeStruct((n, D_MODEL), F32),
        compiler_params=_cparams(("arbitrary",)),
        name="combine_ln",
    )(tile_base, x1, offb, endb, g.reshape(1, -1), b.reshape(1, -1), z)


def _inproj1_kernel(x_ref, w_ref, tab_ref, gq_ref, gk_ref, seg_ref, q_ref, k_ref, v_ref):
    tm = x_ref.shape[0]
    sub = min(tm, MM_ROWS)
    sh = HEAD_DIM // 4
    seg = seg_ref[...]
    lane = lax.broadcasted_iota(jnp.int32, (1, LANES), 1)
    ones_col = jnp.where(lane == HEAD_DIM, 1.0, 0.0).astype(F32)
    for r in range(tm // sub):
        rows = slice(r * sub, (r + 1) * sub)
        h = jnp.dot(x_ref[rows, :].astype(BF16), w_ref[...], preferred_element_type=F32)

        def norm_rope(xc, gain, t0):
            ms = jnp.dot((xc * xc).astype(BF16), seg, preferred_element_type=F32)
            xn = xc * lax.rsqrt(ms + RMS_EPS) * gain
            return _rope_chunk(xn, tab_ref[t0, rows, :], tab_ref[t0 + 1, rows, :], tab_ref[t0 + 2, rows, :], sh)

        for c in range(C_Q // LANES):
            xc = h[:, c * LANES:(c + 1) * LANES]
            q_ref[rows, c * LANES:(c + 1) * LANES] = norm_rope(xc, gq_ref[...], 0).astype(BF16)
        for c in range(C_KV // LANES):
            xc = h[:, C_Q + c * LANES:C_Q + (c + 1) * LANES]
            kc = norm_rope(xc, gk_ref[...], 3).astype(BF16)
            k_ref[2 * c, rows, :] = kc[:, :HEAD_DIM]
            k_ref[2 * c + 1, rows, :] = kc[:, HEAD_DIM:]
            vc = h[:, C_Q + C_KV + c * LANES:C_Q + C_KV + (c + 1) * LANES]
            v_ref[2 * c, rows, :] = jnp.where(lane < HEAD_DIM, vc, ones_col).astype(BF16)
            v_ref[2 * c + 1, rows, :] = jnp.where(lane < HEAD_DIM, pltpu.roll(vc, HEAD_DIM, axis=1),
                                                  ones_col).astype(BF16)


def _inproj1(x, w, tab, gq, gk, S):
    n = x.shape[0]
    tm = _row_tile(S, PROJ_TILE)
    nt = S // tm
    lane_head = np.arange(LANES) // HEAD_DIM
    seg = jnp.asarray((lane_head[:, None] == lane_head[None, :]) / HEAD_DIM, BF16)
    tile2 = lambda g: jnp.concatenate([g, g]).reshape(1, LANES)
    const = lambda shape: pl.BlockSpec(shape, lambda i: (0,) * len(shape))
    return pl.pallas_call(
        _inproj1_kernel,
        grid=(n // tm,),
        in_specs=[
            pl.BlockSpec((tm, D_MODEL), lambda i: (i, 0)),
            const((D_MODEL, C_IN)),
            pl.BlockSpec((6, tm, LANES), lambda i: (0, i % nt, 0)),
            const((1, LANES)), const((1, LANES)), const((LANES, LANES)),
        ],
        out_specs=[
            pl.BlockSpec((tm, C_Q), lambda i: (i, 0)),
            pl.BlockSpec((AX_KV_HEADS, tm, HEAD_DIM), lambda i: (0, i, 0)),
            pl.BlockSpec((AX_KV_HEADS, tm, LANES), lambda i: (0, i, 0)),
        ],
        out_shape=[
            jax.ShapeDtypeStruct((n, C_Q), BF16),
            jax.ShapeDtypeStruct((AX_KV_HEADS, n, HEAD_DIM), BF16),
            jax.ShapeDtypeStruct((AX_KV_HEADS, n, LANES), BF16),
        ],
        compiler_params=_cparams(("parallel",)),
        name="inproj1",
    )(x, w, tab, tile2(gq), tile2(gk), seg)


def _axattn_kernel(q_ref, k_ref, v_ref, o_ref):
    G = AX_HEADS // AX_KV_HEADS
    TQ = q_ref.shape[0]
    chains = [(slice(r * AX_CHAIN_ROWS, (r + 1) * AX_CHAIN_ROWS), slice(j * HEAD_DIM, (j + 1) * HEAD_DIM))
              for r in range(TQ // AX_CHAIN_ROWS) for j in range(G)]

    def scores(c):
        rows, cols = chains[c]
        return lax.dot_general(q_ref[rows, cols], k_ref[...], (((1,), (1,)), ((), ())),
                               preferred_element_type=F32)

    s_next = scores(0)
    for c, (rows, cols) in enumerate(chains):
        s = s_next
        if c + 1 < len(chains):
            s_next = scores(c + 1)
        m = jnp.max(s, axis=-1, keepdims=True)
        p = jnp.exp2(s - m)
        o = jnp.dot(p.astype(BF16), v_ref[...], preferred_element_type=F32)
        o_ref[rows, cols] = (o[:, :HEAD_DIM] / o[:, HEAD_DIM:HEAD_DIM + 1]).astype(BF16)


def _axial_attention(q, k, v, B, S):
    n = q.shape[0]
    TQ = _row_tile(S, AX_Q_TILE)
    nq = S // TQ
    G = AX_HEADS // AX_KV_HEADS
    qspec = pl.BlockSpec((TQ, G * HEAD_DIM), lambda b, g, i: (b * nq + i, g))
    kspec = pl.BlockSpec((None, None, S, HEAD_DIM), lambda b, g, i: (g, b, 0, 0))
    vspec = pl.BlockSpec((None, None, S, LANES), lambda b, g, i: (g, b, 0, 0))
    return pl.pallas_call(
        _axattn_kernel,
        grid=(B, AX_KV_HEADS, nq),
        in_specs=[qspec, kspec, vspec],
        out_specs=qspec,
        out_shape=jax.ShapeDtypeStruct((n, C_Q), BF16),
        compiler_params=_cparams(("parallel", "parallel", "arbitrary")),
        name="axial_attention",
    )(q, k, v)


def _moe(x1, xb, aff, wg, wu, wd, layer, ln_g, ln_b, n):
    cap = CAPACITY_FACTOR * n // N_EXPERTS
    gate_t, mask_t = _expert_choice(aff.T, cap)
    cnt = jnp.sum(mask_t, axis=0)
    off = jnp.cumsum(cnt) - cnt
    dest_t = off[None, :] + jnp.cumsum(mask_t, axis=0) - mask_t
    idx, gsel, dslot = _compact(mask_t, gate_t, dest_t, cap)
    xe = _gather_rows(xb, idx)
    gb = jnp.broadcast_to(gsel[:, None], (gsel.shape[0], LANES))
    ye = _expert_ffn(xe, gb, wg, wu, wd, layer, cap)
    z = _scatter_rows(ye, dslot)
    return _combine_ln(x1, z, off, cnt, ln_g, ln_b)


def _trunk(x, p):
    B, S, _ = x.shape
    n = B * S
    xf = x.reshape(n, D_MODEL)
    tab_w = _tables_window(S)
    tab_a = _tables_axial(S)
    for layer in range(DEPTH):
        i = layer // 2
        if layer % 2 == 0:
            ug, q, k, v = _inproj0(xf, p['ab_w_in'][i], tab_w, S)
            conv = _conv_branch(ug, p['ab_conv_w'][i], p['ab_conv_b'][i], p['ab_conv_ln_g'][i],
                                p['ab_conv_ln_b'][i], S)
            att = _window_attention(q, k, v, p['ab_sink'][i], B, S)
            acts, w_out = [conv, att], p['ab_w_out'][i]
        else:
            q, k, v = _inproj1(xf, p['c_w_in'][i], tab_a, p['c_q_norm'][i], p['c_k_norm'][i], S)
            k = k.reshape(AX_KV_HEADS, B, S, HEAD_DIM)
            v = v.reshape(AX_KV_HEADS, B, S, LANES)
            acts, w_out = [_axial_attention(q, k, v, B, S)], p['c_w_out'][i]
        x1, xb, aff = _outproj_ln_router(acts, xf, w_out, p['ln1_g'][layer], p['ln1_b'][layer],
                                         p['moe_router'][layer])
        xf = _moe(x1, xb, aff, p['moe_w_gate'], p['moe_w_up'], p['moe_w_down'], layer,
                  p['ln2_g'][layer], p['ln2_b'][layer], n)
    return xf.reshape(B, S, D_MODEL)


def kernel(x_prompt, x_sample, ab_w_in, ab_conv_w, ab_conv_b, ab_conv_ln_g, ab_conv_ln_b, ab_sink, ab_w_out, c_w_in, c_q_norm, c_k_norm, c_w_out, ln1_g, ln1_b, moe_router, moe_w_gate, moe_w_up, moe_w_down, ln2_g, ln2_b):
    p = dict(
        ab_w_in=ab_w_in.astype(BF16), ab_conv_w=ab_conv_w, ab_conv_b=ab_conv_b,
        ab_conv_ln_g=ab_conv_ln_g, ab_conv_ln_b=ab_conv_ln_b, ab_sink=ab_sink,
        ab_w_out=ab_w_out.astype(BF16), c_w_in=c_w_in.astype(BF16), c_q_norm=c_q_norm,
        c_k_norm=c_k_norm, c_w_out=c_w_out.astype(BF16), ln1_g=ln1_g, ln1_b=ln1_b,
        moe_router=moe_router, moe_w_gate=moe_w_gate.astype(BF16), moe_w_up=moe_w_up.astype(BF16),
        moe_w_down=moe_w_down.astype(BF16), ln2_g=ln2_g, ln2_b=ln2_b,
    )
    return (_trunk(x_prompt, p), _trunk(x_sample, p))
```
